```python
import math
import jax, jax.numpy as jnp
from jax import lax
import numpy as np

D_MODEL = 4096
BATCH = 4
SEQ = 2048
DEPTH = 4
DEC_BATCH = 128
DEC_SEQ = 1
PAST_LEN = 16384
PAGE_SIZE = 128

N_MIXERS = 3
N_A = (DEPTH + 2) // 3
N_B = (DEPTH + 1) // 3
N_C = DEPTH // 3
EXPAND = 2
D_INNER = EXPAND * D_MODEL
RMS_EPS = 1e-6

A_HEADS = 16
A_DV = D_INNER // A_HEADS
A_DQK = A_DV // 2
A_CHUNK = 64
A_QK = A_HEADS * A_DQK
A_IN = 2 * A_QK + 3 * D_INNER + 2 * A_HEADS

B_GROUP = 16
B_GROUPS = D_INNER // B_GROUP
B_STATE = 64
B_CHUNK = 128
DT_MIN = 0.001
DT_MAX = 0.1

C_BLOCKS = 16
C_BS = D_INNER // C_BLOCKS
C_CONV = 4
C_POW = 8.0

kernel_name = "hybrid_mlstm_s5_rglru_step"


def _rmsnorm(x, g):
    xf = x.astype(jnp.float32)
    y = xf * lax.rsqrt(jnp.mean(xf * xf, axis=-1, keepdims=True) + RMS_EPS)
    return (y * g.astype(jnp.float32)).astype(x.dtype)


def _mlstm_chunk(carry, inp):
    C, n, m = carry
    q, k, v, logi, logf = inp
    lc = q.shape[2]
    b = jnp.cumsum(logf, axis=-1)
    causal = jnp.tril(jnp.ones((lc, lc), dtype=bool))
    dmat = jnp.where(causal, b[..., :, None] - b[..., None, :] + logi[..., None, :], -jnp.inf)
    inter = b + m[..., None]
    m_t = jnp.maximum(inter, jnp.max(dmat, axis=-1))
    s = jnp.einsum('bhtd,bhsd->bhts', q, k) * jnp.exp(dmat - m_t[..., None])
    w_inter = jnp.exp(inter - m_t)
    num = jnp.einsum('bhts,bhsv->bhtv', s, v) + w_inter[..., None] * jnp.einsum('bhvd,bhtd->bhtv', C, q)
    den = jnp.sum(s, axis=-1) + w_inter * jnp.einsum('bhd,bhtd->bht', n, q)
    h = num / jnp.maximum(jnp.abs(den), jnp.exp(-m_t))[..., None]
    b_last = b[..., -1]
    w_s = b_last[..., None] - b + logi
    m_new = jnp.maximum(b_last + m, jnp.max(w_s, axis=-1))
    decay = jnp.exp(b_last + m - m_new)
    w_s = jnp.exp(w_s - m_new[..., None])
    C_new = decay[..., None, None] * C + jnp.einsum('bhs,bhsv,bhsd->bhvd', w_s, v, k)
    n_new = decay[..., None] * n + jnp.einsum('bhs,bhsd->bhd', w_s, k)
    return (C_new, n_new, m_new), h


def _mlstm_mixer(h, C0, n0, m0, w_in, b_if, g_head, w_out):
    f32 = jnp.float32
    bsz, L, _ = h.shape
    proj = h @ w_in
    q, k, v, o, z, gif = jnp.split(
        proj, [A_QK, 2 * A_QK, 2 * A_QK + D_INNER, 2 * A_QK + 2 * D_INNER, 2 * A_QK + 3 * D_INNER], axis=-1)

    def heads(t, d):
        return t.astype(f32).reshape(bsz, L, A_HEADS, d).transpose(0, 2, 1, 3)

    q = heads(q, A_DQK)
    k = heads(k, A_DQK) * (A_DQK ** -0.5)
    v = heads(v, A_DV)
    gif = (gif + b_if).astype(f32).reshape(bsz, L, 2, A_HEADS).transpose(2, 0, 3, 1)
    logi = gif[0]
    logf = jax.nn.log_sigmoid(gif[1])
    lc = math.gcd(L, A_CHUNK)
    nc = L // lc

    def chunks(t):
        t = t.reshape(t.shape[:2] + (nc, lc) + t.shape[3:])
        return jnp.moveaxis(t, 2, 0)

    (C1, n1, m1), hs = lax.scan(
        _mlstm_chunk, (C0.astype(f32), n0.astype(f32), m0.astype(f32)),
        (chunks(q), chunks(k), chunks(v), chunks(logi), chunks(logf)))
    hs = jnp.moveaxis(hs, 0, 2).reshape(bsz, A_HEADS, L, A_DV).transpose(0, 2, 1, 3)
    hs = jax.nn.sigmoid(o.astype(f32)).reshape(bsz, L, A_HEADS, A_DV) * hs
    hs = hs * lax.rsqrt(jnp.mean(hs * hs, axis=-1, keepdims=True) + RMS_EPS)
    hs = hs.reshape(bsz, L, D_INNER) * g_head.astype(f32)
    out = (hs * jax.nn.silu(z.astype(f32))).astype(h.dtype) @ w_out
    return out, C1.astype(h.dtype), n1.astype(h.dtype), m1.astype(h.dtype)


def _cmul(ar, ai, br, bi):
    return ar * br - ai * bi, ar * bi + ai * br


def _s5_combine(e1, e2):
    a1r, a1i, b1r, b1i = e1
    a2r, a2i, b2r, b2i = e2
    ar, ai = _cmul(a2r, a2i, a1r, a1i)
    br, bi = _cmul(a2r, a2i, b1r, b1i)
    return ar, ai, br + b2r, bi + b2i


def _s5_discretize(lam_re, lam_im, log_dt, B_re, B_im):
    f32 = jnp.float32
    lr = jnp.minimum(lam_re.astype(f32), -1e-4)
    li = lam_im.astype(f32)
    dt = jnp.exp(log_dt.astype(f32))[:, None]
    mag = jnp.exp(lr * dt)
    ab_re = mag * jnp.cos(li * dt)
    ab_im = mag * jnp.sin(li * dt)
    nr = ab_re - 1.0
    ni = ab_im
    den = lr * lr + li * li
    q_re = (nr * lr + ni * li) / den
    q_im = (ni * lr - nr * li) / den
    bb_re, bb_im = _cmul(q_re[..., None], q_im[..., None], B_re.astype(f32), B_im.astype(f32))
    return ab_re, ab_im, bb_re, bb_im


def _s5_mixer(h, s_re0, s_im0, w_in, lam_re, lam_im, log_dt, B_re, B_im, C_re, C_im,
              d_skip, w_glu, b_glu, w_out):
    f32 = jnp.float32
    bsz, L, _ = h.shape
    u, z = jnp.split(h @ w_in, 2, axis=-1)
    uf = u.astype(f32)
    ab_re, ab_im, bb_re, bb_im = _s5_discretize(lam_re, lam_im, log_dt, B_re, B_im)
    Cr = C_re.astype(f32)
    Ci = C_im.astype(f32)
    lc = math.gcd(L, B_CHUNK)
    nc = L // lc
    u_ch = jnp.moveaxis(uf.reshape(bsz, nc, lc, B_GROUPS, B_GROUP), 1, 0)

    def body(carry, uc):
        hr, hi = carry
        bu_re = jnp.einsum('blgc,gpc->blgp', uc, bb_re)
        bu_im = jnp.einsum('blgc,gpc->blgp', uc, bb_im)
        pr, pi = _cmul(ab_re, ab_im, hr, hi)
        bu_re = bu_re.at[:, 0].add(pr)
        bu_im = bu_im.at[:, 0].add(pi)
        ar = jnp.broadcast_to(ab_re, bu_re.shape)
        ai = jnp.broadcast_to(ab_im, bu_im.shape)
        _, _, xr, xi = lax.associative_scan(_s5_combine, (ar, ai, bu_re, bu_im), axis=1)
        y = jnp.einsum('blgp,gcp->blgc', xr, Cr) - jnp.einsum('blgp,gcp->blgc', xi, Ci)
        return (xr[:, -1], xi[:, -1]), y

    (hr, hi), ys = lax.scan(body, (s_re0.astype(f32), s_im0.astype(f32)), u_ch)
    y = jnp.moveaxis(ys, 0, 1).reshape(bsz, L, D_INNER) + d_skip.astype(f32) * uf
    g = jax.nn.gelu(y)
    y = g * jax.nn.sigmoid((g.astype(h.dtype) @ w_glu).astype(f32) + b_glu.astype(f32))
    out = (y * jax.nn.silu(z.astype(f32))).astype(h.dtype) @ w_out
    return out, hr.astype(h.dtype), hi.astype(h.dtype)


def _rglru_combine(e1, e2):
    a1, b1 = e1
    a2, b2 = e2
    return a1 * a2, a2 * b1 + b2


def _rglru_mixer(h, h0, conv_buf, w_in, conv_w, conv_b, w_a, b_a, w_x, b_x, lam, w_out):
    f32 = jnp.float32
    bsz, L, _ = h.shape
    u, z = jnp.split(h @ w_in, 2, axis=-1)
    up = jnp.concatenate([conv_buf.astype(u.dtype), u], axis=1)
    new_buf = up[:, L:]
    upf = up.astype(f32)
    cw = conv_w.astype(f32)
    xc = conv_b.astype(f32) + cw[0] * upf[:, 0:L]
    for j in range(1, C_CONV):
        xc = xc + cw[j] * upf[:, j:j + L]
    xb = xc.reshape(bsz, L, C_BLOCKS, C_BS)
    r = jax.nn.sigmoid(jnp.einsum('blnc,ncd->blnd', xb, w_a.astype(f32)).reshape(bsz, L, D_INNER) + b_a.astype(f32))
    i = jax.nn.sigmoid(jnp.einsum('blnc,ncd->blnd', xb, w_x.astype(f32)).reshape(bsz, L, D_INNER) + b_x.astype(f32))
    log_a = -C_POW * r * jax.nn.softplus(-lam.astype(f32))
    a = jnp.exp(log_a)
    bterm = jnp.sqrt(-jnp.expm1(2.0 * log_a)) * (i * xc)
    bterm = bterm.at[:, 0].add(a[:, 0] * h0.astype(f32))
    _, hs = lax.associative_scan(_rglru_combine, (a, bterm), axis=1)
    out = (hs * jax.nn.silu(z.astype(f32))).astype(h.dtype) @ w_out
    return out, hs[:, -1].astype(h.dtype), new_buf


def _trunk(x, c, mC, mn, mm, s_re, s_im, r_h, r_conv, p):
    sc = jax.nn.silu(c)
    nC, nn_, nm, nre, nim, nh, nconv = [], [], [], [], [], [], []
    for i in range(DEPTH):
        ada = (sc @ p['ada_w'][i] + p['ada_b'][i])[:, None, :]
        shift, scale, gate = jnp.split(ada, 3, axis=-1)
        hn = _rmsnorm(x, p['norm_pre'][i]) * (1 + scale) + shift
        kind = i % N_MIXERS
        j = i // N_MIXERS
        if kind == 0:
            out, C1, n1, m1 = _mlstm_mixer(hn, mC[j], mn[j], mm[j], p['mlstm_w_in'][j], p['mlstm_b_if'][j],
                                           p['mlstm_head_norm'][j], p['mlstm_w_out'][j])
            nC.append(C1); nn_.append(n1); nm.append(m1)
        elif kind == 1:
            out, hr, hi = _s5_mixer(hn, s_re[j], s_im[j], p['s5_w_in'][j], p['s5_lambda_re'][j], p['s5_lambda_im'][j],
                                    p['s5_log_dt'][j], p['s5_B_re'][j], p['s5_B_im'][j], p['s5_C_re'][j], p['s5_C_im'][j],
                                    p['s5_D'][j], p['s5_w_glu'][j], p['s5_b_glu'][j], p['s5_w_out'][j])
            nre.append(hr); nim.append(hi)
        else:
            out, h1, buf = _rglru_mixer(hn, r_h[j], r_conv[j], p['rglru_w_in'][j], p['rglru_conv_w'][j], p['rglru_conv_b'][j],
                                        p['rglru_w_a'][j], p['rglru_b_a'][j], p['rglru_w_x'][j], p['rglru_b_x'][j],
                                        p['rglru_lambda'][j], p['rglru_w_out'][j])
            nh.append(h1); nconv.append(buf)
        x = x + gate * _rmsnorm(out, p['norm_post'][i])
    return x, jnp.stack(nC), jnp.stack(nn_), jnp.stack(nm), jnp.stack(nre), jnp.stack(nim), jnp.stack(nh), jnp.stack(nconv)


def setup_inputs(seed: int = 0) -> dict:
    key = jax.random.key(seed)
    ks = list(jax.random.split(key, 64))
    f32 = jnp.float32

    def nk():
        return ks.pop()

    def nrm(shape, s):
        return s * jax.random.normal(nk(), shape, f32)

    inp = {}
    inp['x_prompt'] = nrm((BATCH, SEQ, D_MODEL), 1.0)
    inp['x_sample'] = nrm((DEC_BATCH, DEC_SEQ, D_MODEL), 1.0)
    inp['c_prompt'] = nrm((BATCH, D_MODEL), 1.0)
    inp['c_sample'] = nrm((DEC_BATCH, D_MODEL), 1.0)
    inp['state_mlstm_C'] = nrm((N_A, DEC_BATCH, A_HEADS, A_DV, A_DQK), 0.1)
    inp['state_mlstm_n'] = nrm((N_A, DEC_BATCH, A_HEADS, A_DQK), 0.1)
    inp['state_mlstm_m'] = nrm((N_A, DEC_BATCH, A_HEADS), 1.0)
    inp['state_s5_re'] = nrm((N_B, DEC_BATCH, B_GROUPS, B_STATE), 0.3)
    inp['state_s5_im'] = nrm((N_B, DEC_BATCH, B_GROUPS, B_STATE), 0.3)
    inp['state_rglru_h'] = nrm((N_C, DEC_BATCH, D_INNER), 0.5)
    inp['state_rglru_conv'] = nrm((N_C, DEC_BATCH, C_CONV - 1, D_INNER), 1.0)
    inp['norm_pre'] = 1.0 + nrm((DEPTH, D_MODEL), 0.05)
    inp['norm_post'] = 1.0 + nrm((DEPTH, D_MODEL), 0.05)
    inp['ada_w'] = nrm((DEPTH, D_MODEL, 3 * D_MODEL), 0.5 * D_MODEL ** -0.5)
    inp['ada_b'] = nrm((DEPTH, 3 * D_MODEL), 0.02)
    inp['mlstm_w_in'] = nrm((N_A, D_MODEL, A_IN), D_MODEL ** -0.5)
    b_i = nrm((N_A, A_HEADS), 0.1)
    b_f = jnp.linspace(3.0, 6.0, A_HEADS, dtype=f32)[None] + nrm((N_A, A_HEADS), 0.1)
    inp['mlstm_b_if'] = jnp.concatenate([b_i, b_f], axis=-1)
    inp['mlstm_head_norm'] = 1.0 + nrm((N_A, D_INNER), 0.05)
    inp['mlstm_w_out'] = nrm((N_A, D_INNER, D_MODEL), D_INNER ** -0.5)
    inp['s5_w_in'] = nrm((N_B, D_MODEL, 2 * D_INNER), D_MODEL ** -0.5)
    inp['s5_lambda_re'] = -0.5 + nrm((N_B, B_GROUPS, B_STATE), 0.01)
    inp['s5_lambda_im'] = math.pi * jnp.arange(B_STATE, dtype=f32) + nrm((N_B, B_GROUPS, B_STATE), 0.01)
    inp['s5_log_dt'] = jax.random.uniform(nk(), (N_B, B_GROUPS), f32, math.log(DT_MIN), math.log(DT_MAX))
    inp['s5_B_re'] = nrm((N_B, B_GROUPS, B_STATE, B_GROUP), (2 * B_GROUP) ** -0.5)
    inp['s5_B_im'] = nrm((N_B, B_GROUPS, B_STATE, B_GROUP), (2 * B_GROUP) ** -0.5)
    inp['s5_C_re'] = nrm((N_B, B_GROUPS, B_GROUP, B_STATE), (2 * B_STATE) ** -0.5)
    inp['s5_C_im'] = nrm((N_B, B_GROUPS, B_GROUP, B_STATE), (2 * B_STATE) ** -0.5)
    inp['s5_D'] = nrm((N_B, D_INNER), 1.0)
    inp['s5_w_glu'] = nrm((N_B, D_INNER, D_INNER), D_INNER ** -0.5)
    inp['s5_b_glu'] = nrm((N_B, D_INNER), 0.02)
    inp['s5_w_out'] = nrm((N_B, D_INNER, D_MODEL), D_INNER ** -0.5)
    inp['rglru_w_in'] = nrm((N_C, D_MODEL, 2 * D_INNER), D_MODEL ** -0.5)
    inp['rglru_conv_w'] = nrm((N_C, C_CONV, D_INNER), C_CONV ** -0.5)
    inp['rglru_conv_b'] = nrm((N_C, D_INNER), 0.02)
    inp['rglru_w_a'] = nrm((N_C, C_BLOCKS, C_BS, C_BS), C_BS ** -0.5)
    inp['rglru_b_a'] = nrm((N_C, D_INNER), 0.02)
    inp['rglru_w_x'] = nrm((N_C, C_BLOCKS, C_BS, C_BS), C_BS ** -0.5)
    inp['rglru_b_x'] = nrm((N_C, D_INNER), 0.02)
    a0 = jax.random.uniform(nk(), (N_C, D_INNER), f32, 0.9, 0.999)
    s = a0 ** (1.0 / C_POW)
    inp['rglru_lambda'] = jnp.log(s) - jnp.log1p(-s)
    inp['rglru_w_out'] = nrm((N_C, D_INNER, D_MODEL), D_INNER ** -0.5)
    return inp


def reference(x_prompt, x_sample, c_prompt, c_sample, state_mlstm_C, state_mlstm_n, state_mlstm_m,
              state_s5_re, state_s5_im, state_rglru_h, state_rglru_conv,
              norm_pre, norm_post, ada_w, ada_b,
              mlstm_w_in, mlstm_b_if, mlstm_head_norm, mlstm_w_out,
              s5_w_in, s5_lambda_re, s5_lambda_im, s5_log_dt, s5_B_re, s5_B_im, s5_C_re, s5_C_im,
              s5_D, s5_w_glu, s5_b_glu, s5_w_out,
              rglru_w_in, rglru_conv_w, rglru_conv_b, rglru_w_a, rglru_b_a, rglru_w_x, rglru_b_x,
              rglru_lambda, rglru_w_out):
    p = dict(norm_pre=norm_pre, norm_post=norm_post, ada_w=ada_w, ada_b=ada_b,
             mlstm_w_in=mlstm_w_in, mlstm_b_if=mlstm_b_if, mlstm_head_norm=mlstm_head_norm, mlstm_w_out=mlstm_w_out,
             s5_w_in=s5_w_in, s5_lambda_re=s5_lambda_re, s5_lambda_im=s5_lambda_im, s5_log_dt=s5_log_dt,
             s5_B_re=s5_B_re, s5_B_im=s5_B_im, s5_C_re=s5_C_re, s5_C_im=s5_C_im, s5_D=s5_D,
             s5_w_glu=s5_w_glu, s5_b_glu=s5_b_glu, s5_w_out=s5_w_out,
             rglru_w_in=rglru_w_in, rglru_conv_w=rglru_conv_w, rglru_conv_b=rglru_conv_b,
             rglru_w_a=rglru_w_a, rglru_b_a=rglru_b_a, rglru_w_x=rglru_w_x, rglru_b_x=rglru_b_x,
             rglru_lambda=rglru_lambda, rglru_w_out=rglru_w_out)
    dt = x_prompt.dtype
    bp = x_prompt.shape[0]
    (y_prompt, mC_p, mn_p, mm_p, sre_p, sim_p, rh_p, rconv_p) = _trunk(
        x_prompt, c_prompt,
        jnp.zeros((N_A, bp, A_HEADS, A_DV, A_DQK), dt), jnp.zeros((N_A, bp, A_HEADS, A_DQK), dt),
        jnp.zeros((N_A, bp, A_HEADS), dt),
        jnp.zeros((N_B, bp, B_GROUPS, B_STATE), dt), jnp.zeros((N_B, bp, B_GROUPS, B_STATE), dt),
        jnp.zeros((N_C, bp, D_INNER), dt), jnp.zeros((N_C, bp, C_CONV - 1, D_INNER), dt), p)
    (y_sample, mC_s, mn_s, mm_s, sre_s, sim_s, rh_s, rconv_s) = _trunk(
        x_sample, c_sample, state_mlstm_C, state_mlstm_n, state_mlstm_m, state_s5_re, state_s5_im,
        state_rglru_h, state_rglru_conv, p)
    return (y_prompt, y_sample, mC_p, mn_p, mm_p, sre_p, sim_p, rh_p, rconv_p,
            mC_s, mn_s, mm_s, sre_s, sim_s, rh_s, rconv_s)
```

```python
import functools
import math

import jax
import jax.numpy as jnp
from jax import lax
from jax.experimental import pallas as pl
from jax.experimental.pallas import tpu as pltpu

F32 = jnp.float32
BF16 = jnp.bfloat16
RMS_EPS = 1e-6
DT_FLOOR_LAMBDA = -1e-4
RGLRU_POW = 8.0
LANES = 128
S5_TILE_GROUPS = 8
S5_CHUNK = 16
MLSTM_CHUNK = 256
VMEM_LIMIT = 48 * 1024 * 1024

_NT = (((1,), (1,)), ((), ()))
_TN = (((0,), (0,)), ((), ()))


def _cparams(sem):
    return pltpu.CompilerParams(dimension_semantics=sem, vmem_limit_bytes=VMEM_LIMIT)


def _silu(x):
    return x * jax.nn.sigmoid(x)


def _log_sigmoid(x):
    return jnp.minimum(x, 0.0) - jnp.log1p(jnp.exp(-jnp.abs(x)))


def _softplus(x):
    return jnp.maximum(x, 0.0) + jnp.log1p(jnp.exp(-jnp.abs(x)))


def _gelu_tanh(x):
    c = math.sqrt(2.0 / math.pi)
    return 0.5 * x * (1.0 + jnp.tanh(c * (x + 0.044715 * (x * x * x))))


def _pick(n, pref):
    if n <= pref:
        return n
    t = pref
    while n % t:
        t //= 2
    return t


def _mm_kernel(x_ref, w_ref, *rest, epilogue, n_extra):
    extra = rest[:n_extra]
    o_ref = rest[n_extra]
    acc_ref = rest[n_extra + 1]
    k = pl.program_id(2)
    part = jnp.dot(x_ref[...].astype(BF16), w_ref[...].astype(BF16),
                   preferred_element_type=F32)

    @pl.when(k == 0)
    def _():
        acc_ref[...] = part

    @pl.when(k != 0)
    def _():
        acc_ref[...] += part

    @pl.when(k == pl.num_programs(2) - 1)
    def _():
        acc = acc_ref[...]
        if epilogue is None:
            o_ref[...] = acc.astype(o_ref.dtype)
        else:
            o_ref[...] = epilogue(acc, *[e[...] for e in extra]).astype(o_ref.dtype)


def _matmul(x, w, layer, *, n_out=None, tm=1024, tn=1024, tk=1024, out_dtype=F32,
            epilogue=None, extras=()):
    M, K = x.shape
    N = w.shape[2] if n_out is None else n_out
    tm, tn, tk = _pick(M, tm), _pick(N, tn), _pick(K, tk)
    in_specs = [
        pl.BlockSpec((tm, tk), lambda i, j, k: (i, k)),
        pl.BlockSpec((None, tk, tn), lambda i, j, k: (layer, k, j)),
    ]
    args = [x, w]
    for arr, kind, off in extras:
        if kind == "tile":
            in_specs.append(pl.BlockSpec((tm, tn), lambda i, j, k, off=off: (i, j + off // tn)))
        else:
            in_specs.append(pl.BlockSpec((1, tn), lambda i, j, k, off=off: (0, j + off // tn)))
        args.append(arr)
    kern = functools.partial(_mm_kernel, epilogue=epilogue, n_extra=len(extras))
    return pl.pallas_call(
        kern,
        grid=(M // tm, N // tn, K // tk),
        in_specs=in_specs,
        out_specs=pl.BlockSpec((tm, tn), lambda i, j, k: (i, j)),
        out_shape=jax.ShapeDtypeStruct((M, N), out_dtype),
        scratch_shapes=[pltpu.VMEM((tm, tn), F32)],
        compiler_params=_cparams(("parallel", "parallel", "arbitrary")),
    )(*args)


def _ada_kernel(c_ref, w_ref, b_ref, o_ref, acc_ref):
    k = pl.program_id(2)
    part = jnp.dot(_silu(c_ref[...]).astype(BF16), w_ref[...].astype(BF16),
                   preferred_element_type=F32)

    @pl.when(k == 0)
    def _():
        acc_ref[...] = part

    @pl.when(k != 0)
    def _():
        acc_ref[...] += part

    @pl.when(k == pl.num_programs(2) - 1)
    def _():
        o_ref[...] = acc_ref[...] + b_ref[...]


def _ada_all(c, ada_w, ada_b):
    S, D = c.shape
    depth, _, N = ada_w.shape
    tn, tk = _pick(N, 1024), _pick(D, 2048)
    return pl.pallas_call(
        _ada_kernel,
        grid=(depth, N // tn, D // tk),
        in_specs=[
            pl.BlockSpec((S, tk), lambda l, j, k: (0, k)),
            pl.BlockSpec((None, tk, tn), lambda l, j, k: (l, k, j)),
            pl.BlockSpec((None, 1, tn), lambda l, j, k: (l, 0, j)),
        ],
        out_specs=pl.BlockSpec((None, S, tn), lambda l, j, k: (l, 0, j)),
        out_shape=jax.ShapeDtypeStruct((depth, S, N), F32),
        scratch_shapes=[pltpu.VMEM((S, tn), F32)],
        compiler_params=_cparams(("parallel", "parallel", "arbitrary")),
    )(c, ada_w, ada_b.reshape(depth, 1, N))


def _rms(x, g):
    return x * lax.rsqrt(jnp.mean(x * x, axis=-1, keepdims=True) + RMS_EPS) * g


def _prenorm_kernel(x_ref, g_ref, shift_ref, scale_ref, o_ref):
    y = _rms(x_ref[...], g_ref[...])
    o_ref[...] = (y * (1.0 + scale_ref[...]) + shift_ref[...]).astype(o_ref.dtype)


def _postnorm_kernel(x_ref, out_ref, g_ref, gate_ref, o_ref):
    o_ref[...] = x_ref[...] + gate_ref[...] * _rms(out_ref[...], g_ref[...])


def _ada_specs(ada, tl, D):
    la = ada.shape[1]
    if la == 1:
        return [pl.BlockSpec((None, 1, D), lambda b, t, c=c: (b, 0, c)) for c in range(3)]
    return [pl.BlockSpec((None, tl, D), lambda b, t, c=c: (b, t, c)) for c in range(3)]


def _prenorm(x, g, layer, ada):
    Bx, Lx, D = x.shape
    tl = _pick(Lx, 256)
    shift_spec, scale_spec, _ = _ada_specs(ada, tl, D)
    return pl.pallas_call(
        _prenorm_kernel,
        grid=(Bx, Lx // tl),
        in_specs=[
            pl.BlockSpec((None, tl, D), lambda b, t: (b, t, 0)),
            pl.BlockSpec((None, 1, D), lambda b, t: (layer, 0, 0)),
            shift_spec, scale_spec,
        ],
        out_specs=pl.BlockSpec((None, tl, D), lambda b, t: (b, t, 0)),
        out_shape=jax.ShapeDtypeStruct((Bx, Lx, D), BF16),
        compiler_params=_cparams(("parallel", "parallel")),
    )(x, g.reshape(g.shape[0], 1, D), ada, ada)


def _postnorm(x, out, g, layer, ada):
    Bx, Lx, D = x.shape
    tl = _pick(Lx, 256)
    gate_spec = _ada_specs(ada, tl, D)[2]
    return pl.pallas_call(
        _postnorm_kernel,
        grid=(Bx, Lx // tl),
        in_specs=[
            pl.BlockSpec((None, tl, D), lambda b, t: (b, t, 0)),
            pl.BlockSpec((None, tl, D), lambda b, t: (b, t, 0)),
            pl.BlockSpec((None, 1, D), lambda b, t: (layer, 0, 0)),
            gate_spec,
        ],
        out_specs=pl.BlockSpec((None, tl, D), lambda b, t: (b, t, 0)),
        out_shape=jax.ShapeDtypeStruct((Bx, Lx, D), F32),
        compiler_params=_cparams(("parallel", "parallel")),
    )(x, out.reshape(Bx, Lx, D), g.reshape(g.shape[0], 1, D), ada)


def _mlstm_gate_kernel(g_ref, bias_ref, colli_ref, colb_ref, rowli_ref, rowb_ref):
    g = g_ref[...] + bias_ref[...]
    lc = g.shape[0]
    lf = _log_sigmoid(g)
    tri = (lax.broadcasted_iota(jnp.int32, (lc, lc), 0)
           >= lax.broadcasted_iota(jnp.int32, (lc, lc), 1)).astype(F32)
    bcum = jnp.dot(tri, lf, preferred_element_type=F32, precision=lax.Precision.HIGHEST)
    colli_ref[...] = g
    colb_ref[...] = bcum
    rowli_ref[...] = g.T
    rowb_ref[...] = bcum.T


def _mlstm_gates(gates, bias, lc):
    B, L, W = gates.shape
    col = jax.ShapeDtypeStruct((B, L, W), F32)
    row = jax.ShapeDtypeStruct((B, W, L), F32)
    cspec = pl.BlockSpec((None, lc, W), lambda b, c: (b, c, 0))
    rspec = pl.BlockSpec((None, W, lc), lambda b, c: (b, 0, c))
    return pl.pallas_call(
        _mlstm_gate_kernel,
        grid=(B, L // lc),
        in_specs=[cspec, pl.BlockSpec((1, W), lambda b, c: (0, 0))],
        out_specs=[cspec, cspec, rspec, rspec],
        out_shape=[col, col, row, row],
        compiler_params=_cparams(("parallel", "parallel")),
    )(gates, bias)


def _head_gate(hh, o, z, gh):
    hs = jax.nn.sigmoid(o) * hh
    hs = hs * lax.rsqrt(jnp.mean(hs * hs, axis=-1, keepdims=True) + RMS_EPS)
    return (hs * gh * _silu(z)).astype(BF16)


def _mlstm_prompt_kernel(q_ref, k_ref, v_ref, o_ref, z_ref, colli_ref, colb_ref,
                         rowli_ref, rowb_ref, gh_ref,
                         out_ref, cst_ref, nst_ref, mst_ref,
                         c_s, n_s, m_s, *, heads):
    h = pl.program_id(1)
    c = pl.program_id(2)
    lc, dqk = q_ref.shape

    @pl.when(c == 0)
    def _():
        c_s[...] = jnp.zeros_like(c_s)
        n_s[...] = jnp.zeros_like(n_s)
        m_s[...] = jnp.zeros_like(m_s)

    q = q_ref[...]
    k = k_ref[...] * (dqk ** -0.5)
    v = v_ref[...]
    lane = lax.broadcasted_iota(jnp.int32, colb_ref.shape, 1)
    b_col = jnp.sum(jnp.where(lane == heads + h, colb_ref[...], 0.0), axis=1, keepdims=True)
    li_col = jnp.sum(jnp.where(lane == h, colli_ref[...], 0.0), axis=1, keepdims=True)
    b_row = rowb_ref[pl.ds(heads + h, 1), :]
    li_row = rowli_ref[pl.ds(h, 1), :]
    m_prev = m_s[...]

    causal = (lax.broadcasted_iota(jnp.int32, (lc, lc), 1)
              <= lax.broadcasted_iota(jnp.int32, (lc, lc), 0))
    dmat = jnp.where(causal, b_col - b_row + li_row, -jnp.inf)
    inter = b_col + m_prev
    m_t = jnp.maximum(inter, jnp.max(dmat, axis=1, keepdims=True))
    qb, kb, vb = q.astype(BF16), k.astype(BF16), v.astype(BF16)
    s = lax.dot_general(qb, kb, _NT, preferred_element_type=F32) * jnp.exp(dmat - m_t)
    w_inter = jnp.exp(inter - m_t)
    cmat = c_s[...]
    qc = lax.dot_general(qb, cmat.astype(BF16), _NT, preferred_element_type=F32)
    num = jnp.dot(s.astype(BF16), vb, preferred_element_type=F32) + w_inter * qc
    nvec = n_s[...]
    den = (jnp.sum(s, axis=1, keepdims=True)
           + w_inter * jnp.sum(q * nvec, axis=1, keepdims=True))
    hh = num / jnp.maximum(jnp.abs(den), jnp.exp(-m_t))
    out_ref[...] = _head_gate(hh, o_ref[...], z_ref[...], gh_ref[...])

    b_last = b_col[lc - 1:lc, :]
    m_new = jnp.maximum(b_last + m_prev,
                        jnp.max(b_last - b_row + li_row, axis=1, keepdims=True))
    decay = jnp.exp(b_last + m_prev - m_new)
    w_col = jnp.exp(b_last - b_col + li_col - m_new)
    c_new = decay * cmat + lax.dot_general((v * w_col).astype(BF16), kb, _TN,
                                           preferred_element_type=F32)
    n_new = decay * nvec + jnp.sum(k * w_col, axis=0, keepdims=True)
    c_s[...] = c_new
    n_s[...] = n_new
    m_s[...] = m_new

    @pl.when(c == pl.num_programs(2) - 1)
    def _():
        cst_ref[...] = c_new
        nst_ref[...] = n_new
        mst_ref[...] = jnp.broadcast_to(m_new, mst_ref.shape)


def _mlstm_prompt(proj, gates, bias, g_head, layer, heads, dqk, dv):
    B, L, _ = proj.shape
    lc = _pick(L, MLSTM_CHUNK)
    colli, colb, rowli, rowb = _mlstm_gates(gates, bias, lc)
    W = gates.shape[2]
    kq = heads * dqk // dqk
    kv = 2 * heads * dqk // dv
    tok = lambda blk, off: pl.BlockSpec((None, lc, blk), lambda b, h, c, off=off: (b, c, off + h))
    cspec = pl.BlockSpec((None, lc, W), lambda b, h, c: (b, c, 0))
    rspec = pl.BlockSpec((None, W, lc), lambda b, h, c: (b, 0, c))
    kern = functools.partial(_mlstm_prompt_kernel, heads=heads)
    return pl.pallas_call(
        kern,
        grid=(B, heads, L // lc),
        in_specs=[
            tok(dqk, 0), tok(dqk, kq), tok(dv, kv), tok(dv, kv + heads), tok(dv, kv + 2 * heads),
            cspec, cspec, rspec, rspec,
            pl.BlockSpec((None, 1, dv), lambda b, h, c: (layer, 0, h)),
        ],
        out_specs=[
            pl.BlockSpec((None, lc, dv), lambda b, h, c: (b, c, h)),
            pl.BlockSpec((None, None, dv, dqk), lambda b, h, c: (b, h, 0, 0)),
            pl.BlockSpec((None, None, 1, dqk), lambda b, h, c: (b, h, 0, 0)),
            pl.BlockSpec((None, None, 1, LANES), lambda b, h, c: (b, h, 0, 0)),
        ],
        out_shape=[
            jax.ShapeDtypeStruct((B, L, heads * dv), BF16),
            jax.ShapeDtypeStruct((B, heads, dv, dqk), F32),
            jax.ShapeDtypeStruct((B, heads, 1, dqk), F32),
            jax.ShapeDtypeStruct((B, heads, 1, LANES), F32),
        ],
        scratch_shapes=[pltpu.VMEM((dv, dqk), F32), pltpu.VMEM((1, dqk), F32),
                        pltpu.VMEM((1, 1), F32)],
        compiler_params=_cparams(("parallel", "parallel", "arbitrary")),
    )(proj, proj, proj, proj, proj, colli, colb, rowli, rowb,
      g_head.reshape(g_head.shape[0], 1, heads * dv))


def _mlstm_step_coef_kernel(proj_ref, g_ref, bias_ref, n_ref, m_ref,
                            s_ref, wi_ref, mnew_ref, dn_ref, nnew_ref, kw_ref, *, heads, dqk):
    g = g_ref[...] + bias_ref[...]
    lf = _log_sigmoid(g)
    m = m_ref[...]
    for h in range(heads):
        q = proj_ref[:, h * dqk:(h + 1) * dqk]
        k = proj_ref[:, (heads + h) * dqk:(heads + h + 1) * dqk] * (dqk ** -0.5)
        nv = n_ref[:, h * dqk:(h + 1) * dqk]
        logi = g[:, h:h + 1]
        inter = lf[:, heads + h:heads + h + 1] + m[:, h:h + 1]
        m_t = jnp.maximum(inter, logi)
        w_s = jnp.exp(logi - m_t)
        w_inter = jnp.exp(inter - m_t)
        s = jnp.sum(q * k, axis=1, keepdims=True) * w_s
        den = s + w_inter * jnp.sum(nv * q, axis=1, keepdims=True)
        s_ref[:, h:h + 1] = s
        wi_ref[:, h:h + 1] = w_inter
        mnew_ref[:, h:h + 1] = m_t
        dn_ref[:, h:h + 1] = jnp.maximum(jnp.abs(den), jnp.exp(-m_t))
        nnew_ref[:, h * dqk:(h + 1) * dqk] = w_inter * nv + w_s * k
        kw_ref[:, h * dqk:(h + 1) * dqk] = w_s * k


def _mlstm_step_state_kernel(decay_ref, c_ref, qt_ref, kw_ref, vt_ref, cnew_ref, cq_ref,
                             *, bb):
    h = pl.program_id(0)
    bi = pl.program_id(1)

    @pl.when(bi == 0)
    def _():
        cq_ref[...] = jnp.zeros_like(cq_ref)

    qt = qt_ref[...].astype(BF16)
    kw = kw_ref[...].astype(BF16)
    vt = vt_ref[...]
    lane = lax.broadcasted_iota(jnp.int32, vt.shape, 1)
    acc = jnp.zeros(vt.shape, F32)
    for i in range(bb):
        b = bi * bb + i
        cmat = c_ref[i]
        sel = lane == b
        r = jnp.dot(cmat.astype(BF16), qt, preferred_element_type=F32)
        acc = acc + jnp.where(sel, r, 0.0)
        outer = jnp.dot(jnp.where(sel, vt, 0.0).astype(BF16), kw, preferred_element_type=F32)
        cnew_ref[i] = decay_ref[h, b] * cmat + outer
    cq_ref[...] += acc


def _mlstm_step_out_kernel(s_ref, wi_ref, dn_ref, cq_ref, v_ref, o_ref, z_ref, gh_ref, out_ref):
    h = pl.program_id(0)
    lane = lax.broadcasted_iota(jnp.int32, s_ref.shape, 1)
    col = lambda ref: jnp.sum(jnp.where(lane == h, ref[...], 0.0), axis=1, keepdims=True)
    hh = (col(s_ref) * v_ref[...] + col(wi_ref) * cq_ref[...]) / col(dn_ref)
    out_ref[...] = _head_gate(hh, o_ref[...], z_ref[...], gh_ref[...])


def _mlstm_step(proj, gates, bias, g_head, layer, c_state, n_state, m_state, heads, dqk, dv):
    B = proj.shape[0]
    W = gates.shape[1]
    qk_w = heads * dqk
    small = jax.ShapeDtypeStruct((B, heads), F32)
    wide = jax.ShapeDtypeStruct((B, qk_w), F32)
    full = lambda shp: pl.BlockSpec(shp, lambda i: (0,) * len(shp))
    s, w_inter, m_new, denom, n_new, kw = pl.pallas_call(
        functools.partial(_mlstm_step_coef_kernel, heads=heads, dqk=dqk),
        grid=(1,),
        in_specs=[full((B, 2 * qk_w)), full((B, W)), full((1, W)), full((B, qk_w)),
                  full((B, heads))],
        out_specs=[full((B, heads))] * 4 + [full((B, qk_w))] * 2,
        out_shape=[small] * 4 + [wide] * 2,
        compiler_params=_cparams(("arbitrary",)),
    )(proj, gates, bias, n_state[layer].reshape(B, qk_w), m_state[layer])

    q_t = proj[:, :qk_w].reshape(B, heads, dqk).transpose(1, 2, 0)
    v_t = proj[:, 2 * qk_w:2 * qk_w + heads * dv].reshape(B, heads, dv).transpose(1, 2, 0)
    kw_h = kw.reshape(B, heads, dqk).transpose(1, 0, 2)
    bb = _pick(B, 8)
    c_new, cq_t = pl.pallas_call(
        functools.partial(_mlstm_step_state_kernel, bb=bb),
        grid=(heads, B // bb),
        in_specs=[
            pl.BlockSpec(memory_space=pltpu.SMEM),
            pl.BlockSpec((None, bb, None, dv, dqk), lambda h, i: (layer, i, h, 0, 0)),
            pl.BlockSpec((None, dqk, B), lambda h, i: (h, 0, 0)),
            pl.BlockSpec((None, B, dqk), lambda h, i: (h, 0, 0)),
            pl.BlockSpec((None, dv, B), lambda h, i: (h, 0, 0)),
        ],
        out_specs=[
            pl.BlockSpec((bb, None, dv, dqk), lambda h, i: (i, h, 0, 0)),
            pl.BlockSpec((None, dv, B), lambda h, i: (h, 0, 0)),
        ],
        out_shape=[jax.ShapeDtypeStruct((B, heads, dv, dqk), F32),
                   jax.ShapeDtypeStruct((heads, dv, B), F32)],
        compiler_params=_cparams(("parallel", "arbitrary")),
    )(w_inter.T, c_state, q_t, kw_h, v_t)

    cq = cq_t.transpose(2, 0, 1).reshape(B, heads * dv)
    kv = 2 * qk_w // dv
    hblk = lambda off: pl.BlockSpec((B, dv), lambda h, off=off: (0, off + h))
    sm = pl.BlockSpec((B, heads), lambda h: (0, 0))
    gated = pl.pallas_call(
        _mlstm_step_out_kernel,
        grid=(heads,),
        in_specs=[sm, sm, sm, hblk(0), hblk(kv), hblk(kv + heads), hblk(kv + 2 * heads),
                  pl.BlockSpec((None, 1, dv), lambda h: (layer, 0, h))],
        out_specs=hblk(0),
        out_shape=jax.ShapeDtypeStruct((B, heads * dv), BF16),
        compiler_params=_cparams(("parallel",)),
    )(s, w_inter, denom, cq, proj, proj, proj, g_head.reshape(g_head.shape[0], 1, heads * dv))
    return gated, c_new, n_new.reshape(B, heads, dqk), m_new


def _s5_params(lam_re, lam_im, log_dt, b_re, b_im, c_re, c_im, T):
    G, P = lam_re.shape
    GC = b_re.shape[2]
    tg = S5_TILE_GROUPS
    nt = G // tg
    lr = jnp.minimum(lam_re.astype(F32), DT_FLOOR_LAMBDA)
    li = lam_im.astype(F32)
    dt = jnp.exp(log_dt.astype(F32))[:, None]
    mag = jnp.exp(lr * dt)
    ar = mag * jnp.cos(li * dt)
    ai = mag * jnp.sin(li * dt)
    nr, ni = ar - 1.0, ai
    den = lr * lr + li * li
    q_re = (nr * lr + ni * li) / den
    q_im = (ni * lr - nr * li) / den
    bbr = q_re[..., None] * b_re.astype(F32) - q_im[..., None] * b_im.astype(F32)
    bbi = q_re[..., None] * b_im.astype(F32) + q_im[..., None] * b_re.astype(F32)
    pows = [(jnp.ones_like(ar), jnp.zeros_like(ai))]
    for _ in range(T):
        pr, pi = pows[-1]
        pows.append((pr * ar - pi * ai, pr * ai + pi * ar))
    cr, ci = c_re.astype(F32), c_im.astype(F32)

    def tile_rows(x):
        return x.reshape(nt, tg * GC, x.shape[-1])

    pd, qtd, rmat = [], [], []
    eye = jnp.eye(tg, dtype=F32)
    hp = lax.Precision.HIGHEST
    for j in range(T):
        pr, pi = pows[T - 1 - j]
        xr = pr[..., None] * bbr - pi[..., None] * bbi
        xi = pr[..., None] * bbi + pi[..., None] * bbr
        pd.append(tile_rows(jnp.concatenate([xr.transpose(0, 2, 1), xi.transpose(0, 2, 1)], -1)))
    for l in range(T):
        pr, pi = pows[l + 1]
        er = cr * pr[:, None, :] - ci * pi[:, None, :]
        ei = cr * pi[:, None, :] + ci * pr[:, None, :]
        qtd.append(tile_rows(jnp.concatenate([er, -ei], -1)))
    for t in range(T - 1, -1, -1):
        pr, pi = pows[t]
        er = cr * pr[:, None, :] - ci * pi[:, None, :]
        ei = cr * pi[:, None, :] + ci * pr[:, None, :]
        kt = (jnp.einsum('gcp,gpd->gcd', er, bbr, precision=hp)
              - jnp.einsum('gcp,gpd->gcd', ei, bbi, precision=hp))
        kt = kt.reshape(nt, tg, GC, GC)
        bd = jnp.einsum('ngcd,gh->ngdhc', kt, eye)
        rmat.append(bd.reshape(nt, tg * GC, tg * GC))
    pr, pi = pows[T]
    a1 = jnp.concatenate([pr, pr], -1).reshape(nt, 1, tg * 2 * P)
    a2 = jnp.concatenate([-pi, pi], -1).reshape(nt, 1, tg * 2 * P)
    return (jnp.concatenate(pd, 1), jnp.concatenate(qtd, 1), jnp.concatenate(rmat, 1), a1, a2)


def _s5_kernel(u_ref, x0_ref, pd_ref, qtd_ref, r_ref, a1_ref, a2_ref, d_ref,
               g32_ref, xf_ref,
               u2_s, e_s, xp_s, *, T, nb, nc):
    rows = nb * nc
    tw = u_ref.shape[1]
    sw = x0_ref.shape[1]
    ng = sw // tw
    for j in range(T):
        u2_s[:, j * tw:(j + 1) * tw] = u_ref[pl.ds(j, rows, stride=T), :].astype(BF16)

    def expand(dense):
        rg = (lax.broadcasted_iota(jnp.int32, dense.shape, 0) % tw) // (tw // ng)
        return jnp.concatenate(
            [jnp.where(rg == g, dense, 0.0).astype(BF16) for g in range(ng)], axis=1)

    e = jnp.dot(u2_s[...], expand(pd_ref[...]), preferred_element_type=F32)
    for g in range(ng):
        e_s[g] = e[:, g * tw:(g + 1) * tw]

    a1 = a1_ref[...]
    a2 = a2_ref[...]
    half = tw // 2

    def step(kk, x):
        sl = pl.ds(kk, nb, stride=nc)
        new = []
        for g in range(ng):
            gs = slice(g * tw, (g + 1) * tw)
            xg = x[:, gs]
            xp_s[g, sl, :] = xg
            new.append(a1[:, gs] * xg + a2[:, gs] * pltpu.roll(xg, half, 1) + e_s[g, sl, :])
        return jnp.concatenate(new, axis=1)

    xf_ref[...] = lax.fori_loop(0, nc, step, x0_ref[...])

    xprev = jnp.concatenate([xp_s[g].astype(BF16) for g in range(ng)], axis=1)
    yx = lax.dot_general(xprev, expand(qtd_ref[...]), _NT,
                         preferred_element_type=F32)
    for l in range(T):
        loc = jnp.dot(u2_s[:, :(l + 1) * tw], r_ref[(T - 1 - l) * tw:, :].astype(BF16),
                      preferred_element_type=F32)
        sl = pl.ds(l, rows, stride=T)
        y = yx[:, l * tw:(l + 1) * tw] + loc + d_ref[...] * u_ref[sl, :]
        g32_ref[sl, :] = _gelu_tanh(y)


def _s5_scan(uz, x0, params, d_skip, nb, T):
    M = uz.shape[0]
    pd, qtd, rmat, a1, a2 = params
    nt = pd.shape[0]
    tw = LANES
    sw = x0.shape[1] // nt
    nc = M // (nb * T)
    rows = nb * nc
    kern = functools.partial(_s5_kernel, T=T, nb=nb, nc=nc)
    E = nt * tw
    return pl.pallas_call(
        kern,
        grid=(nt,),
        in_specs=[
            pl.BlockSpec((M, tw), lambda i: (0, i)),
            pl.BlockSpec((nb, sw), lambda i: (0, i)),
            pl.BlockSpec((None, T * tw, tw), lambda i: (i, 0, 0)),
            pl.BlockSpec((None, T * tw, tw), lambda i: (i, 0, 0)),
            pl.BlockSpec((None, T * tw, tw), lambda i: (i, 0, 0)),
            pl.BlockSpec((None, 1, sw), lambda i: (i, 0, 0)),
            pl.BlockSpec((None, 1, sw), lambda i: (i, 0, 0)),
            pl.BlockSpec((1, tw), lambda i: (0, i)),
        ],
        out_specs=[
            pl.BlockSpec((M, tw), lambda i: (0, i)),
            pl.BlockSpec((nb, sw), lambda i: (0, i)),
        ],
        out_shape=[
            jax.ShapeDtypeStruct((M, E), F32),
            jax.ShapeDtypeStruct((nb, nt * sw), F32),
        ],
        scratch_shapes=[pltpu.VMEM((rows, T * tw), BF16), pltpu.VMEM((sw // tw, rows, tw), F32),
                        pltpu.VMEM((sw // tw, rows, tw), F32)],
        compiler_params=_cparams(("parallel",)),
    )(uz, x0, pd, qtd, rmat, a1, a2, d_skip)


def _glu_epilogue(acc, g32, z, bias):
    return g32 * jax.nn.sigmoid(acc + bias) * _silu(z)


def _rglru_gates(xc, wa_ref, wx_ref, ba_ref, bx_ref, lam_ref):
    xb = xc.astype(BF16)
    r = jax.nn.sigmoid(jnp.dot(xb, wa_ref[...].astype(BF16), preferred_element_type=F32)
                       + ba_ref[...])
    i = jax.nn.sigmoid(jnp.dot(xb, wx_ref[...].astype(BF16), preferred_element_type=F32)
                       + bx_ref[...])
    log_a = -RGLRU_POW * r * _softplus(-lam_ref[...])
    a = jnp.exp(log_a)
    return a, jnp.sqrt(-jnp.tanh(log_a) * (a * a + 1.0)) * (i * xc)


def _rglru_prompt_kernel(u_ref, z_ref, prev_ref, wa_ref, wx_ref, cw_ref, cb_ref, ba_ref, bx_ref,
                         lam_ref, out_ref, hlast_ref,
                         ext_s, a_s, b_s, hs_s, h_s, *, nb, tl, halo):
    t = pl.program_id(1)
    cbw = u_ref.shape[2]

    @pl.when(t == 0)
    def _():
        h_s[...] = jnp.zeros_like(h_s)
        ext_s[:, 0:halo, :] = jnp.zeros((nb, halo, cbw), F32)

    @pl.when(t != 0)
    def _():
        ext_s[:, 0:halo, :] = prev_ref[...]

    u = u_ref[...]
    ext_s[:, halo:, :] = u
    cw = cw_ref[...]
    nk = cw.shape[0]
    xc = cb_ref[...] + cw[nk - 1:nk, :] * u
    for j in range(nk - 1):
        xc = xc + cw[j:j + 1, :] * ext_s[:, pl.ds(halo - (nk - 1) + j, tl), :]
    xc = xc.reshape(nb * tl, cbw)
    a, b = _rglru_gates(xc, wa_ref, wx_ref, ba_ref, bx_ref, lam_ref)
    nl = cbw // LANES
    for g in range(nl):
        a_s[g] = a[:, g * LANES:(g + 1) * LANES]
        b_s[g] = b[:, g * LANES:(g + 1) * LANES]

    def step(r, hcur):
        sl = pl.ds(r, nb, stride=tl)
        new = []
        for g in range(nl):
            hg = a_s[g, sl, :] * hcur[:, g * LANES:(g + 1) * LANES] + b_s[g, sl, :]
            hs_s[g, sl, :] = hg
            new.append(hg)
        return jnp.concatenate(new, axis=1)

    hfin = lax.fori_loop(0, tl, step, h_s[...])
    h_s[...] = hfin
    hlast_ref[...] = hfin
    hs = jnp.concatenate([hs_s[g] for g in range(nl)], axis=1)
    out_ref[...] = (hs.reshape(nb, tl, cbw) * _silu(z_ref[...])).astype(BF16)


def _rglru_prompt(uz3, w_a, w_x, conv_w, conv_b, b_a, b_x, lam):
    B, L, E2 = uz3.shape
    E = E2 // 2
    nblk, cbw = w_a.shape[1], w_a.shape[2]
    tl = _pick(L, 256)
    halo = 8
    nk = conv_w.shape[1]
    kern = functools.partial(_rglru_prompt_kernel, nb=B, tl=tl, halo=halo)
    vec = lambda a: a.reshape(1, E)
    vspec = pl.BlockSpec((1, cbw), lambda n, t: (0, n))
    out, hlast = pl.pallas_call(
        kern,
        grid=(nblk, L // tl),
        in_specs=[
            pl.BlockSpec((B, tl, cbw), lambda n, t: (0, t, n)),
            pl.BlockSpec((B, tl, cbw), lambda n, t: (0, t, nblk + n)),
            pl.BlockSpec((B, halo, cbw), lambda n, t: (0, jnp.maximum(t * (tl // halo) - 1, 0), n)),
            pl.BlockSpec((None, None, cbw, cbw), lambda n, t: (0, n, 0, 0)),
            pl.BlockSpec((None, None, cbw, cbw), lambda n, t: (0, n, 0, 0)),
            pl.BlockSpec((None, nk, cbw), lambda n, t: (0, 0, n)),
            vspec, vspec, vspec, vspec,
        ],
        out_specs=[
            pl.BlockSpec((B, tl, cbw), lambda n, t: (0, t, n)),
            pl.BlockSpec((B, cbw), lambda n, t: (0, n)),
        ],
        out_shape=[jax.ShapeDtypeStruct((B, L, E), BF16), jax.ShapeDtypeStruct((B, E), F32)],
        scratch_shapes=[pltpu.VMEM((B, tl + halo, cbw), F32)]
        + [pltpu.VMEM((cbw // LANES, B * tl, LANES), F32)] * 3 + [
                        pltpu.VMEM((B, cbw), F32)],
        compiler_params=_cparams(("parallel", "arbitrary")),
    )(uz3, uz3, uz3, w_a, w_x, conv_w, vec(conv_b), vec(b_a), vec(b_x), vec(lam))
    return out, hlast


def _rglru_step_kernel(u_ref, z_ref, buf_ref, h0_ref, wa_ref, wx_ref, cw_ref, cb_ref, ba_ref,
                       bx_ref, lam_ref, out_ref, hnew_ref):
    u = u_ref[...]
    cw = cw_ref[...]
    nk = cw.shape[0]
    xc = cb_ref[...] + cw[nk - 1:nk, :] * u
    for j in range(nk - 1):
        xc = xc + cw[j:j + 1, :] * buf_ref[j]
    a, b = _rglru_gates(xc, wa_ref, wx_ref, ba_ref, bx_ref, lam_ref)
    hnew = a * h0_ref[...] + b
    hnew_ref[...] = hnew
    out_ref[...] = (hnew * _silu(z_ref[...])).astype(BF16)


def _rglru_step(uz, buf_t, h0, w_a, w_x, conv_w, conv_b, b_a, b_x, lam):
    B, E2 = uz.shape
    E = E2 // 2
    nblk, cbw = w_a.shape[1], w_a.shape[2]
    nk = conv_w.shape[1]
    vec = lambda a: a.reshape(1, E)
    vspec = pl.BlockSpec((1, cbw), lambda n: (0, n))
    blk = lambda off: pl.BlockSpec((B, cbw), lambda n, off=off: (0, off + n))
    return pl.pallas_call(
        _rglru_step_kernel,
        grid=(nblk,),
        in_specs=[
            blk(0), blk(nblk),
            pl.BlockSpec((nk - 1, B, cbw), lambda n: (0, 0, n)),
            blk(0),
            pl.BlockSpec((None, None, cbw, cbw), lambda n: (0, n, 0, 0)),
            pl.BlockSpec((None, None, cbw, cbw), lambda n: (0, n, 0, 0)),
            pl.BlockSpec((None, nk, cbw), lambda n: (0, 0, n)),
            vspec, vspec, vspec, vspec,
        ],
        out_specs=[blk(0), blk(0)],
        out_shape=[jax.ShapeDtypeStruct((B, E), BF16), jax.ShapeDtypeStruct((B, E), F32)],
        compiler_params=_cparams(("parallel",)),
    )(uz, uz, buf_t, h0, w_a, w_x, conv_w, vec(conv_b), vec(b_a), vec(b_x), vec(lam))


def _trunk(x, ada_all, states, p, is_prompt):
    B, L, D = x.shape
    M = B * L
    depth = p['ada_w'].shape[0]
    mC, mn, mm, s_re, s_im, r_h, r_conv = states
    heads, dv, dqk = mC.shape[2], mC.shape[3], mC.shape[4]
    G, P = s_re.shape[2], s_re.shape[3]
    E = p['s5_D'].shape[1]
    x3 = x if is_prompt else x.reshape(1, M, D)
    outs = dict(C=[], n=[], m=[], re=[], im=[], h=[], conv=[])
    for i in range(depth):
        ada = ada_all[i]
        ada = ada.reshape(B, 1, 3 * D) if is_prompt else ada.reshape(1, M, 3 * D)
        hn = _prenorm(x3, p['norm_pre'], i, ada).reshape(M, D)
        kind, j = i % 3, i // 3
        if kind == 0:
            w_in = p['mlstm_w_in']
            n_main = 2 * heads * dqk + 3 * heads * dv
            proj = _matmul(hn, w_in, j, n_out=n_main)
            w_if = jnp.pad(w_in[j, :, n_main:], ((0, 0), (0, LANES - 2 * heads)))[None]
            gates = _matmul(hn, w_if, 0)
            bias = jnp.pad(p['mlstm_b_if'][j], (0, LANES - 2 * heads)).reshape(1, LANES)
            if is_prompt:
                gated, c1, n1, m1 = _mlstm_prompt(
                    proj.reshape(B, L, n_main), gates.reshape(B, L, LANES), bias,
                    p['mlstm_head_norm'], j, heads, dqk, dv)
                gated = gated.reshape(M, heads * dv)
                n1 = n1.reshape(B, heads, dqk)
                m1 = m1[:, :, 0, 0]
            else:
                gated, c1, n1, m1 = _mlstm_step(proj, gates, bias, p['mlstm_head_norm'], j,
                                                mC, mn, mm, heads, dqk, dv)
            outs['C'].append(c1); outs['n'].append(n1); outs['m'].append(m1)
            out = _matmul(gated, p['mlstm_w_out'], j)
        elif kind == 1:
            uz = _matmul(hn, p['s5_w_in'], j)
            T = S5_CHUNK if L % S5_CHUNK == 0 else 1
            params = _s5_params(p['s5_lambda_re'][j], p['s5_lambda_im'][j], p['s5_log_dt'][j],
                                p['s5_B_re'][j], p['s5_B_im'][j], p['s5_C_re'][j], p['s5_C_im'][j], T)
            if is_prompt:
                x0 = jnp.zeros((B, G * 2 * P), F32)
            else:
                x0 = jnp.stack([s_re[j], s_im[j]], axis=2).reshape(B, G * 2 * P)
            g32, xf = _s5_scan(uz, x0, params, p['s5_D'][j].reshape(1, E), B, T)
            xf = xf.reshape(B, G, 2, P)
            outs['re'].append(xf[:, :, 0]); outs['im'].append(xf[:, :, 1])
            y = _matmul(g32, p['s5_w_glu'], j, out_dtype=BF16, epilogue=_glu_epilogue,
                        extras=[(g32, 'tile', 0), (uz, 'tile', E),
                                (p['s5_b_glu'][j].reshape(1, E), 'row', 0)])
            out = _matmul(y, p['s5_w_out'], j)
        else:
            uz = _matmul(hn, p['rglru_w_in'], j)
            args = (p['rglru_w_a'][j:j + 1], p['rglru_w_x'][j:j + 1], p['rglru_conv_w'][j:j + 1],
                    p['rglru_conv_b'][j], p['rglru_b_a'][j], p['rglru_b_x'][j], p['rglru_lambda'][j])
            if is_prompt:
                gated, h1 = _rglru_prompt(uz.reshape(B, L, 2 * E), *args)
                gated = gated.reshape(M, E)
                nk = p['rglru_conv_w'].shape[1]
                buf = uz.reshape(B, L, 2 * E)[:, L - (nk - 1):, :E]
            else:
                buf_t = r_conv[j].transpose(1, 0, 2)
                gated, h1 = _rglru_step(uz, buf_t, r_h[j], *args)
                buf = jnp.concatenate([r_conv[j][:, 1:], uz[:, None, :E]], axis=1)
            outs['h'].append(h1); outs['conv'].append(buf)
            out = _matmul(gated, p['rglru_w_out'], j)
        x3 = _postnorm(x3, out, p['norm_post'], i, ada)
    st = lambda k: jnp.stack(outs[k])
    return (x3.reshape(B, L, D), st('C'), st('n'), st('m'), st('re'), st('im'), st('h'), st('conv'))


def kernel(x_prompt, x_sample, c_prompt, c_sample, state_mlstm_C, state_mlstm_n, state_mlstm_m,
           state_s5_re, state_s5_im, state_rglru_h, state_rglru_conv,
           norm_pre, norm_post, ada_w, ada_b,
           mlstm_w_in, mlstm_b_if, mlstm_head_norm, mlstm_w_out,
           s5_w_in, s5_lambda_re, s5_lambda_im, s5_log_dt, s5_B_re, s5_B_im, s5_C_re, s5_C_im,
           s5_D, s5_w_glu, s5_b_glu, s5_w_out,
           rglru_w_in, rglru_conv_w, rglru_conv_b, rglru_w_a, rglru_b_a, rglru_w_x, rglru_b_x,
           rglru_lambda, rglru_w_out):
    p = dict(norm_pre=norm_pre, norm_post=norm_post, ada_w=ada_w, ada_b=ada_b,
             mlstm_w_in=mlstm_w_in, mlstm_b_if=mlstm_b_if, mlstm_head_norm=mlstm_head_norm,
             mlstm_w_out=mlstm_w_out,
             s5_w_in=s5_w_in, s5_lambda_re=s5_lambda_re, s5_lambda_im=s5_lambda_im,
             s5_log_dt=s5_log_dt, s5_B_re=s5_B_re, s5_B_im=s5_B_im, s5_C_re=s5_C_re,
             s5_C_im=s5_C_im, s5_D=s5_D, s5_w_glu=s5_w_glu, s5_b_glu=s5_b_glu, s5_w_out=s5_w_out,
             rglru_w_in=rglru_w_in, rglru_conv_w=rglru_conv_w, rglru_conv_b=rglru_conv_b,
             rglru_w_a=rglru_w_a, rglru_b_a=rglru_b_a, rglru_w_x=rglru_w_x, rglru_b_x=rglru_b_x,
             rglru_lambda=rglru_lambda, rglru_w_out=rglru_w_out)
    bp = x_prompt.shape[0]
    bs = x_sample.shape[0]
    s_all = bp + bs
    s_pad = -(-s_all // 16) * 16
    c_all = jnp.pad(jnp.concatenate([c_prompt, c_sample], axis=0), ((0, s_pad - s_all), (0, 0)))
    ada_all = _ada_all(c_all, ada_w, ada_b)
    states = (state_mlstm_C, state_mlstm_n, state_mlstm_m, state_s5_re, state_s5_im,
              state_rglru_h, state_rglru_conv)
    res_p = _trunk(x_prompt, ada_all[:, :bp], states, p, True)
    res_s = _trunk(x_sample, ada_all[:, bp:s_all], states, p, False)
    return (res_p[0], res_s[0]) + tuple(res_p[1:]) + tuple(res_s[1:])
```

```python
import functools
import math

import jax
import jax.numpy as jnp
from jax import lax
from jax.experimental import pallas as pl
from jax.experimental.pallas import tpu as pltpu

F32 = jnp.float32
BF16 = jnp.bfloat16
RMS_EPS = 1e-6
DT_FLOOR_LAMBDA = -1e-4
RGLRU_POW = 8.0
LANES = 128
S5_TILE_GROUPS = 8
S5_CHUNK = 16
MLSTM_CHUNK = 256
V7X_VMEM_BYTES = 64 * 1024 * 1024
VMEM_LIMIT = V7X_VMEM_BYTES * 7 // 8
MM_X_PANEL_BYTES = V7X_VMEM_BYTES // 4
MM_W_PANEL_BYTES = V7X_VMEM_BYTES // 8

_NT = (((1,), (1,)), ((), ()))
_TN = (((0,), (0,)), ((), ()))


def _cparams(sem):
    return pltpu.CompilerParams(dimension_semantics=sem, vmem_limit_bytes=VMEM_LIMIT)


def _silu(x):
    return x * jax.nn.sigmoid(x)


def _log_sigmoid(x):
    return jnp.minimum(x, 0.0) - jnp.log1p(jnp.exp(-jnp.abs(x)))


def _softplus(x):
    return jnp.maximum(x, 0.0) + jnp.log1p(jnp.exp(-jnp.abs(x)))


def _gelu_tanh(x):
    c = math.sqrt(2.0 / math.pi)
    return 0.5 * x * (1.0 + jnp.tanh(c * (x + 0.044715 * (x * x * x))))


def _pick(n, pref):
    if n <= pref:
        return n
    t = pref
    while n % t:
        t //= 2
    return t


def _mm_kernel(x_ref, w_ref, *rest, epilogue, n_extra):
    extra = rest[:n_extra]
    o_ref = rest[n_extra]
    acc = jnp.dot(x_ref[...], w_ref[...].astype(BF16), preferred_element_type=F32)
    if epilogue is not None:
        acc = epilogue(acc, *[e[...] for e in extra])
    o_ref[...] = acc.astype(o_ref.dtype)


def _mm_tiles(M, K, N):
    tm = _pick(M, max(MM_X_PANEL_BYTES // (2 * K), 16))
    tn = _pick(N, max(MM_W_PANEL_BYTES // (4 * K), LANES))
    return tm, tn


def _matmul(x, w, layer, *, n_out=None, out_dtype=F32, epilogue=None, extras=(), name="mm"):
    assert x.dtype == BF16
    M, K = x.shape
    N = w.shape[2] if n_out is None else n_out
    tm, tn = _mm_tiles(M, K, N)
    in_specs = [
        pl.BlockSpec((tm, K), lambda i, j: (i, 0), pipeline_mode=pl.Buffered(1)),
        pl.BlockSpec((None, K, tn), lambda i, j: (layer, 0, j)),
    ]
    args = [x, w]
    for arr, kind, off in extras:
        if kind == "tile":
            in_specs.append(pl.BlockSpec((tm, tn), lambda i, j, off=off: (i, j + off // tn)))
        else:
            in_specs.append(pl.BlockSpec((1, tn), lambda i, j, off=off: (0, j + off // tn)))
        args.append(arr)
    kern = functools.partial(_mm_kernel, epilogue=epilogue, n_extra=len(extras))
    return pl.pallas_call(
        kern,
        grid=(M // tm, N // tn),
        in_specs=in_specs,
        out_specs=pl.BlockSpec((tm, tn), lambda i, j: (i, j)),
        out_shape=jax.ShapeDtypeStruct((M, N), out_dtype),
        compiler_params=_cparams(("parallel", "arbitrary")),
        name=name,
    )(*args)


def _ada_kernel(c_ref, w_ref, b_ref, o_ref, acc_ref):
    k = pl.program_id(2)
    part = jnp.dot(_silu(c_ref[...]).astype(BF16), w_ref[...].astype(BF16),
                   preferred_element_type=F32)

    @pl.when(k == 0)
    def _():
        acc_ref[...] = part

    @pl.when(k != 0)
    def _():
        acc_ref[...] += part

    @pl.when(k == pl.num_programs(2) - 1)
    def _():
        o_ref[...] = acc_ref[...] + b_ref[...]


def _ada_all(c, ada_w, ada_b):
    S, D = c.shape
    depth, _, N = ada_w.shape
    tn, tk = _pick(N, 1024), _pick(D, 2048)
    return pl.pallas_call(
        _ada_kernel,
        grid=(depth, N // tn, D // tk),
        in_specs=[
            pl.BlockSpec((S, tk), lambda l, j, k: (0, k)),
            pl.BlockSpec((None, tk, tn), lambda l, j, k: (l, k, j)),
            pl.BlockSpec((None, 1, tn), lambda l, j, k: (l, 0, j)),
        ],
        out_specs=pl.BlockSpec((None, S, tn), lambda l, j, k: (l, 0, j)),
        out_shape=jax.ShapeDtypeStruct((depth, S, N), F32),
        scratch_shapes=[pltpu.VMEM((S, tn), F32)],
        compiler_params=_cparams(("parallel", "parallel", "arbitrary")),
        name="adaln",
    )(c, ada_w, ada_b.reshape(depth, 1, N))


def _rms(x, g):
    return x * lax.rsqrt(jnp.mean(x * x, axis=-1, keepdims=True) + RMS_EPS) * g


def _prenorm_kernel(x_ref, g_ref, shift_ref, scale_ref, o_ref):
    y = _rms(x_ref[...], g_ref[...])
    o_ref[...] = (y * (1.0 + scale_ref[...]) + shift_ref[...]).astype(o_ref.dtype)


def _postnorm_kernel(x_ref, out_ref, g_ref, gate_ref, o_ref):
    o_ref[...] = x_ref[...] + gate_ref[...] * _rms(out_ref[...], g_ref[...])


def _ada_specs(ada, tl, D):
    la = ada.shape[1]
    if la == 1:
        return [pl.BlockSpec((None, 1, D), lambda b, t, c=c: (b, 0, c)) for c in range(3)]
    return [pl.BlockSpec((None, tl, D), lambda b, t, c=c: (b, t, c)) for c in range(3)]


def _prenorm(x, g, layer, ada):
    Bx, Lx, D = x.shape
    tl = _pick(Lx, 256)
    shift_spec, scale_spec, _ = _ada_specs(ada, tl, D)
    return pl.pallas_call(
        _prenorm_kernel,
        grid=(Bx, Lx // tl),
        in_specs=[
            pl.BlockSpec((None, tl, D), lambda b, t: (b, t, 0)),
            pl.BlockSpec((None, 1, D), lambda b, t: (layer, 0, 0)),
            shift_spec, scale_spec,
        ],
        out_specs=pl.BlockSpec((None, tl, D), lambda b, t: (b, t, 0)),
        out_shape=jax.ShapeDtypeStruct((Bx, Lx, D), BF16),
        compiler_params=_cparams(("parallel", "parallel")),
        name="prenorm",
    )(x, g.reshape(g.shape[0], 1, D), ada, ada)


def _postnorm(x, out, g, layer, ada):
    Bx, Lx, D = x.shape
    tl = _pick(Lx, 256)
    gate_spec = _ada_specs(ada, tl, D)[2]
    return pl.pallas_call(
        _postnorm_kernel,
        grid=(Bx, Lx // tl),
        in_specs=[
            pl.BlockSpec((None, tl, D), lambda b, t: (b, t, 0)),
            pl.BlockSpec((None, tl, D), lambda b, t: (b, t, 0)),
            pl.BlockSpec((None, 1, D), lambda b, t: (layer, 0, 0)),
            gate_spec,
        ],
        out_specs=pl.BlockSpec((None, tl, D), lambda b, t: (b, t, 0)),
        out_shape=jax.ShapeDtypeStruct((Bx, Lx, D), F32),
        compiler_params=_cparams(("parallel", "parallel")),
        name="postnorm",
    )(x, out.reshape(Bx, Lx, D), g.reshape(g.shape[0], 1, D), ada)


def _mlstm_gate_kernel(g_ref, bias_ref, colli_ref, colb_ref, rowli_ref, rowb_ref):
    g = g_ref[...] + bias_ref[...]
    lc = g.shape[0]
    lf = _log_sigmoid(g)
    tri = (lax.broadcasted_iota(jnp.int32, (lc, lc), 0)
           >= lax.broadcasted_iota(jnp.int32, (lc, lc), 1)).astype(F32)
    bcum = jnp.dot(tri, lf, preferred_element_type=F32, precision=lax.Precision.HIGHEST)
    colli_ref[...] = g
    colb_ref[...] = bcum
    rowli_ref[...] = g.T
    rowb_ref[...] = bcum.T


def _mlstm_gates(gates, bias, lc):
    B, L, W = gates.shape
    col = jax.ShapeDtypeStruct((B, L, W), F32)
    row = jax.ShapeDtypeStruct((B, W, L), F32)
    cspec = pl.BlockSpec((None, lc, W), lambda b, c: (b, c, 0))
    rspec = pl.BlockSpec((None, W, lc), lambda b, c: (b, 0, c))
    return pl.pallas_call(
        _mlstm_gate_kernel,
        grid=(B, L // lc),
        in_specs=[cspec, pl.BlockSpec((1, W), lambda b, c: (0, 0))],
        out_specs=[cspec, cspec, rspec, rspec],
        out_shape=[col, col, row, row],
        compiler_params=_cparams(("parallel", "parallel")),
        name="mlstm_gates",
    )(gates, bias)


def _head_gate(hh, o, z, gh):
    hs = jax.nn.sigmoid(o) * hh
    hs = hs * lax.rsqrt(jnp.mean(hs * hs, axis=-1, keepdims=True) + RMS_EPS)
    return (hs * gh * _silu(z)).astype(BF16)


def _mlstm_prompt_kernel(q_ref, k_ref, v_ref, o_ref, z_ref, colli_ref, colb_ref,
                         rowli_ref, rowb_ref, gh_ref,
                         out_ref, cst_ref, nst_ref, mst_ref,
                         c_s, n_s, m_s, *, heads):
    h = pl.program_id(1)
    c = pl.program_id(2)
    lc, dqk = q_ref.shape

    @pl.when(c == 0)
    def _():
        c_s[...] = jnp.zeros_like(c_s)
        n_s[...] = jnp.zeros_like(n_s)
        m_s[...] = jnp.zeros_like(m_s)

    q = q_ref[...]
    k = k_ref[...] * (dqk ** -0.5)
    v = v_ref[...]
    lane = lax.broadcasted_iota(jnp.int32, colb_ref.shape, 1)
    b_col = jnp.sum(jnp.where(lane == heads + h, colb_ref[...], 0.0), axis=1, keepdims=True)
    li_col = jnp.sum(jnp.where(lane == h, colli_ref[...], 0.0), axis=1, keepdims=True)
    b_row = rowb_ref[pl.ds(heads + h, 1), :]
    li_row = rowli_ref[pl.ds(h, 1), :]
    m_prev = m_s[...]

    causal = (lax.broadcasted_iota(jnp.int32, (lc, lc), 1)
              <= lax.broadcasted_iota(jnp.int32, (lc, lc), 0))
    dmat = jnp.where(causal, b_col - b_row + li_row, -jnp.inf)
    inter = b_col + m_prev
    m_t = jnp.maximum(inter, jnp.max(dmat, axis=1, keepdims=True))
    qb, kb, vb = q.astype(BF16), k.astype(BF16), v.astype(BF16)
    s = lax.dot_general(qb, kb, _NT, preferred_element_type=F32) * jnp.exp(dmat - m_t)
    w_inter = jnp.exp(inter - m_t)
    cmat = c_s[...]
    qc = lax.dot_general(qb, cmat.astype(BF16), _NT, preferred_element_type=F32)
    num = jnp.dot(s.astype(BF16), vb, preferred_element_type=F32) + w_inter * qc
    nvec = n_s[...]
    den = (jnp.sum(s, axis=1, keepdims=True)
           + w_inter * jnp.sum(q * nvec, axis=1, keepdims=True))
    hh = num / jnp.maximum(jnp.abs(den), jnp.exp(-m_t))
    out_ref[...] = _head_gate(hh, o_ref[...], z_ref[...], gh_ref[...])

    b_last = b_col[lc - 1:lc, :]
    m_new = jnp.maximum(b_last + m_prev,
                        jnp.max(b_last - b_row + li_row, axis=1, keepdims=True))
    decay = jnp.exp(b_last + m_prev - m_new)
    w_col = jnp.exp(b_last - b_col + li_col - m_new)
    c_new = decay * cmat + lax.dot_general((v * w_col).astype(BF16), kb, _TN,
                                           preferred_element_type=F32)
    n_new = decay * nvec + jnp.sum(k * w_col, axis=0, keepdims=True)
    c_s[...] = c_new
    n_s[...] = n_new
    m_s[...] = m_new

    @pl.when(c == pl.num_programs(2) - 1)
    def _():
        cst_ref[...] = c_new
        nst_ref[...] = n_new
        mst_ref[...] = jnp.broadcast_to(m_new, mst_ref.shape)


def _mlstm_prompt(proj, gates, bias, g_head, layer, heads, dqk, dv):
    B, L, _ = proj.shape
    lc = _pick(L, MLSTM_CHUNK)
    colli, colb, rowli, rowb = _mlstm_gates(gates, bias, lc)
    W = gates.shape[2]
    kq = heads * dqk // dqk
    kv = 2 * heads * dqk // dv
    tok = lambda blk, off: pl.BlockSpec((None, lc, blk), lambda b, h, c, off=off: (b, c, off + h))
    cspec = pl.BlockSpec((None, lc, W), lambda b, h, c: (b, c, 0))
    rspec = pl.BlockSpec((None, W, lc), lambda b, h, c: (b, 0, c))
    kern = functools.partial(_mlstm_prompt_kernel, heads=heads)
    return pl.pallas_call(
        kern,
        grid=(B, heads, L // lc),
        in_specs=[
            tok(dqk, 0), tok(dqk, kq), tok(dv, kv), tok(dv, kv + heads), tok(dv, kv + 2 * heads),
            cspec, cspec, rspec, rspec,
            pl.BlockSpec((None, 1, dv), lambda b, h, c: (layer, 0, h)),
        ],
        out_specs=[
            pl.BlockSpec((None, lc, dv), lambda b, h, c: (b, c, h)),
            pl.BlockSpec((None, None, dv, dqk), lambda b, h, c: (b, h, 0, 0)),
            pl.BlockSpec((None, None, 1, dqk), lambda b, h, c: (b, h, 0, 0)),
            pl.BlockSpec((None, None, 1, LANES), lambda b, h, c: (b, h, 0, 0)),
        ],
        out_shape=[
            jax.ShapeDtypeStruct((B, L, heads * dv), BF16),
            jax.ShapeDtypeStruct((B, heads, dv, dqk), F32),
            jax.ShapeDtypeStruct((B, heads, 1, dqk), F32),
            jax.ShapeDtypeStruct((B, heads, 1, LANES), F32),
        ],
        scratch_shapes=[pltpu.VMEM((dv, dqk), F32), pltpu.VMEM((1, dqk), F32),
                        pltpu.VMEM((1, 1), F32)],
        compiler_params=_cparams(("parallel", "parallel", "arbitrary")),
        name="mlstm_prompt",
    )(proj, proj, proj, proj, proj, colli, colb, rowli, rowb,
      g_head.reshape(g_head.shape[0], 1, heads * dv))


def _mlstm_step_coef_kernel(proj_ref, g_ref, bias_ref, n_ref, m_ref,
                            s_ref, wi_ref, mnew_ref, dn_ref, nnew_ref, kw_ref, *, heads, dqk):
    g = g_ref[...] + bias_ref[...]
    lf = _log_sigmoid(g)
    m = m_ref[...]
    for h in range(heads):
        q = proj_ref[:, h * dqk:(h + 1) * dqk]
        k = proj_ref[:, (heads + h) * dqk:(heads + h + 1) * dqk] * (dqk ** -0.5)
        nv = n_ref[:, h * dqk:(h + 1) * dqk]
        logi = g[:, h:h + 1]
        inter = lf[:, heads + h:heads + h + 1] + m[:, h:h + 1]
        m_t = jnp.maximum(inter, logi)
        w_s = jnp.exp(logi - m_t)
        w_inter = jnp.exp(inter - m_t)
        s = jnp.sum(q * k, axis=1, keepdims=True) * w_s
        den = s + w_inter * jnp.sum(nv * q, axis=1, keepdims=True)
        s_ref[:, h:h + 1] = s
        wi_ref[:, h:h + 1] = w_inter
        mnew_ref[:, h:h + 1] = m_t
        dn_ref[:, h:h + 1] = jnp.maximum(jnp.abs(den), jnp.exp(-m_t))
        nnew_ref[:, h * dqk:(h + 1) * dqk] = w_inter * nv + w_s * k
        kw_ref[:, h * dqk:(h + 1) * dqk] = w_s * k


def _mlstm_step_state_kernel(decay_ref, c_ref, qt_ref, kw_ref, vt_ref, *rest, bb, first_layer):
    cnew_ref, cq_ref = rest[-2:]
    if first_layer is not None:
        for other in range(cnew_ref.shape[0]):
            if other != first_layer:
                cnew_ref[other] = jnp.zeros(cnew_ref.shape[1:], F32)
        cnew_ref = cnew_ref.at[first_layer]
    h = pl.program_id(0)
    bi = pl.program_id(1)

    @pl.when(bi == 0)
    def _():
        cq_ref[...] = jnp.zeros_like(cq_ref)

    qt = qt_ref[...].astype(BF16)
    kw = kw_ref[...].astype(BF16)
    vt = vt_ref[...]
    lane = lax.broadcasted_iota(jnp.int32, vt.shape, 1)
    acc = jnp.zeros(vt.shape, F32)
    for i in range(bb):
        b = bi * bb + i
        cmat = c_ref[i]
        sel = lane == b
        r = jnp.dot(cmat.astype(BF16), qt, preferred_element_type=F32)
        acc = acc + jnp.where(sel, r, 0.0)
        outer = jnp.dot(jnp.where(sel, vt, 0.0).astype(BF16), kw, preferred_element_type=F32)
        cnew_ref[i] = decay_ref[h, b] * cmat + outer
    cq_ref[...] += acc


def _mlstm_step_out_kernel(s_ref, wi_ref, dn_ref, cq_ref, v_ref, o_ref, z_ref, gh_ref, out_ref):
    h = pl.program_id(0)
    lane = lax.broadcasted_iota(jnp.int32, s_ref.shape, 1)
    col = lambda ref: jnp.sum(jnp.where(lane == h, ref[...], 0.0), axis=1, keepdims=True)
    hh = (col(s_ref) * v_ref[...] + col(wi_ref) * cq_ref[...]) / col(dn_ref)
    out_ref[...] = _head_gate(hh, o_ref[...], z_ref[...], gh_ref[...])


def _mlstm_step(proj, gates, bias, g_head, layer, c_state, n_state, m_state, heads, dqk, dv,
                c_new_all):
    B = proj.shape[0]
    W = gates.shape[1]
    qk_w = heads * dqk
    small = jax.ShapeDtypeStruct((B, heads), F32)
    wide = jax.ShapeDtypeStruct((B, qk_w), F32)
    full = lambda shp: pl.BlockSpec(shp, lambda i: (0,) * len(shp))
    s, w_inter, m_new, denom, n_new, kw = pl.pallas_call(
        functools.partial(_mlstm_step_coef_kernel, heads=heads, dqk=dqk),
        grid=(1,),
        in_specs=[full((B, 2 * qk_w)), full((B, W)), full((1, W)), full((B, qk_w)),
                  full((B, heads))],
        out_specs=[full((B, heads))] * 4 + [full((B, qk_w))] * 2,
        out_shape=[small] * 4 + [wide] * 2,
        compiler_params=_cparams(("arbitrary",)),
        name="mlstm_step_coef",
    )(proj, gates, bias, n_state[layer].reshape(B, qk_w), m_state[layer])

    q_t = proj[:, :qk_w].reshape(B, heads, dqk).transpose(1, 2, 0)
    v_t = proj[:, 2 * qk_w:2 * qk_w + heads * dv].reshape(B, heads, dv).transpose(1, 2, 0)
    kw_h = kw.reshape(B, heads, dqk).transpose(1, 0, 2)
    bb = _pick(B, 8)
    slab = pl.BlockSpec((None, bb, None, dv, dqk), lambda h, i: (layer, i, h, 0, 0))
    in_specs = [
        pl.BlockSpec(memory_space=pltpu.SMEM),
        slab,
        pl.BlockSpec((None, dqk, B), lambda h, i: (h, 0, 0)),
        pl.BlockSpec((None, B, dqk), lambda h, i: (h, 0, 0)),
        pl.BlockSpec((None, dv, B), lambda h, i: (h, 0, 0)),
    ]
    args = [w_inter.T, c_state, q_t, kw_h, v_t]
    if c_new_all is None:
        aliases, first_layer = {}, layer
        out_slab = pl.BlockSpec((c_state.shape[0], bb, None, dv, dqk), lambda h, i: (0, i, h, 0, 0))
    else:
        in_specs.append(pl.BlockSpec(memory_space=pl.ANY))
        args.append(c_new_all)
        aliases, first_layer = {len(args) - 1: 0}, None
        out_slab = slab
    c_new_all, cq_t = pl.pallas_call(
        functools.partial(_mlstm_step_state_kernel, bb=bb, first_layer=first_layer),
        grid=(heads, B // bb),
        in_specs=in_specs,
        out_specs=[out_slab, pl.BlockSpec((None, dv, B), lambda h, i: (h, 0, 0))],
        out_shape=[jax.ShapeDtypeStruct(c_state.shape, F32),
                   jax.ShapeDtypeStruct((heads, dv, B), F32)],
        input_output_aliases=aliases,
        compiler_params=_cparams(("parallel", "arbitrary")),
        name="mlstm_step_state",
    )(*args)

    cq = cq_t.transpose(2, 0, 1).reshape(B, heads * dv)
    kv = 2 * qk_w // dv
    hblk = lambda off: pl.BlockSpec((B, dv), lambda h, off=off: (0, off + h))
    sm = pl.BlockSpec((B, heads), lambda h: (0, 0))
    gated = pl.pallas_call(
        _mlstm_step_out_kernel,
        grid=(heads,),
        in_specs=[sm, sm, sm, hblk(0), hblk(kv), hblk(kv + heads), hblk(kv + 2 * heads),
                  pl.BlockSpec((None, 1, dv), lambda h: (layer, 0, h))],
        out_specs=hblk(0),
        out_shape=jax.ShapeDtypeStruct((B, heads * dv), BF16),
        compiler_params=_cparams(("parallel",)),
        name="mlstm_step_out",
    )(s, w_inter, denom, cq, proj, proj, proj, g_head.reshape(g_head.shape[0], 1, heads * dv))
    return gated, c_new_all, n_new.reshape(B, heads, dqk), m_new


def _s5_params(lam_re, lam_im, log_dt, b_re, b_im, c_re, c_im, T):
    G, P = lam_re.shape
    GC = b_re.shape[2]
    tg = S5_TILE_GROUPS
    nt = G // tg
    lr = jnp.minimum(lam_re.astype(F32), DT_FLOOR_LAMBDA)
    li = lam_im.astype(F32)
    dt = jnp.exp(log_dt.astype(F32))[:, None]
    mag = jnp.exp(lr * dt)
    ar = mag * jnp.cos(li * dt)
    ai = mag * jnp.sin(li * dt)
    nr, ni = ar - 1.0, ai
    den = lr * lr + li * li
    q_re = (nr * lr + ni * li) / den
    q_im = (ni * lr - nr * li) / den
    bbr = q_re[..., None] * b_re.astype(F32) - q_im[..., None] * b_im.astype(F32)
    bbi = q_re[..., None] * b_im.astype(F32) + q_im[..., None] * b_re.astype(F32)
    t = jnp.arange(T + 1, dtype=F32)[:, None, None]
    mag_t = jnp.exp(t * (lr * dt))
    pw_re = mag_t * jnp.cos(t * (li * dt))
    pw_im = mag_t * jnp.sin(t * (li * dt))
    pw_re, pw_im, bbr, bbi = lax.optimization_barrier((pw_re, pw_im, bbr, bbi))
    cr, ci = c_re.astype(F32)[None], c_im.astype(F32)[None]
    bbr_t, bbi_t = bbr.transpose(0, 2, 1)[None], bbi.transpose(0, 2, 1)[None]

    def tile_rows(x):
        w = x.shape[-1]
        return x.reshape(T, nt, tg * GC, w).transpose(1, 0, 2, 3).reshape(nt, T * tg * GC, w)

    pr, pi = pw_re[:T][::-1, :, None, :], pw_im[:T][::-1, :, None, :]
    pd = tile_rows(jnp.concatenate([pr * bbr_t - pi * bbi_t, pr * bbi_t + pi * bbr_t], -1))
    e_re = cr * pw_re[:, :, None, :] - ci * pw_im[:, :, None, :]
    e_im = cr * pw_im[:, :, None, :] + ci * pw_re[:, :, None, :]
    qtd = tile_rows(jnp.concatenate([e_re[1:], -e_im[1:]], -1))
    hp = lax.Precision.HIGHEST
    kt = (jnp.einsum('tgcp,gpd->tgcd', e_re[:T][::-1], bbr, precision=hp)
          - jnp.einsum('tgcp,gpd->tgcd', e_im[:T][::-1], bbi, precision=hp))
    bd = jnp.einsum('tngcd,gh->ntgdhc', kt.reshape(T, nt, tg, GC, GC), jnp.eye(tg, dtype=F32))
    rmat = bd.reshape(nt, T * tg * GC, tg * GC)
    a1 = jnp.concatenate([pw_re[T], pw_re[T]], -1).reshape(nt, 1, tg * 2 * P)
    a2 = jnp.concatenate([-pw_im[T], pw_im[T]], -1).reshape(nt, 1, tg * 2 * P)
    return pd, qtd, rmat, a1, a2


def _s5_kernel(u_ref, x0_ref, pd_ref, qtd_ref, r_ref, a1_ref, a2_ref, d_ref,
               g32_ref, g16_ref, xf_ref,
               u2_s, e_s, es_s, xp_s, *, T, nb, nc):
    rows = nb * nc
    tw = u_ref.shape[1]
    sw = x0_ref.shape[1]
    ng = sw // tw
    for j in range(T):
        u2_s[:, j * tw:(j + 1) * tw] = u_ref[pl.ds(j, rows, stride=T), :].astype(BF16)

    def expand(dense):
        rg = (lax.broadcasted_iota(jnp.int32, dense.shape, 0) % tw) // (tw // ng)
        return jnp.concatenate(
            [jnp.where(rg == g, dense, 0.0).astype(BF16) for g in range(ng)], axis=1)

    e = jnp.dot(u2_s[...], expand(pd_ref[...]), preferred_element_type=F32)
    half = tw // 2

    def swap(xg):
        return pltpu.roll(xg, half, 1)

    for g in range(ng):
        eg = e[:, g * tw:(g + 1) * tw]
        e_s[g] = eg
        es_s[g] = swap(eg)

    a1 = a1_ref[...]
    a2 = a2_ref[...]

    def step(kk, carry):
        sl = pl.ds(kk, nb, stride=nc)
        new = []
        for g in range(ng):
            gs = slice(g * tw, (g + 1) * tw)
            xg, xsg = carry[g]
            xp_s[g, sl, :] = xg
            new.append((a1[:, gs] * xg + a2[:, gs] * xsg + e_s[g, sl, :],
                        a1[:, gs] * xsg - a2[:, gs] * xg + es_s[g, sl, :]))
        return tuple(new)

    x0 = x0_ref[...]
    init = tuple((x0[:, g * tw:(g + 1) * tw], swap(x0[:, g * tw:(g + 1) * tw])) for g in range(ng))
    fin = lax.fori_loop(0, nc, step, init, unroll=min(nc, 4))
    xf_ref[...] = jnp.concatenate([f[0] for f in fin], axis=1)

    xprev = jnp.concatenate([xp_s[g].astype(BF16) for g in range(ng)], axis=1)
    yx = lax.dot_general(xprev, expand(qtd_ref[...]), _NT,
                         preferred_element_type=F32)

    def emit(l, loc):
        sl = pl.ds(l, rows, stride=T)
        y = yx[:, l * tw:(l + 1) * tw] + loc + d_ref[...] * u_ref[sl, :]
        g32_ref[sl, :] = _gelu_tanh(y)

    if T % 2 == 0:
        for l in range(0, T, 2):
            rhs = jnp.concatenate([r_ref[(T - 1 - l) * tw:(T + 1) * tw, :],
                                   r_ref[(T - 2 - l) * tw:T * tw, :]], axis=1).astype(BF16)
            loc = jnp.dot(u2_s[:, :(l + 2) * tw], rhs, preferred_element_type=F32)
            emit(l, loc[:, :tw])
            emit(l + 1, loc[:, tw:])
    else:
        for l in range(T):
            loc = jnp.dot(u2_s[:, :(l + 1) * tw], r_ref[(T - 1 - l) * tw:T * tw, :].astype(BF16),
                          preferred_element_type=F32)
            emit(l, loc)
    g16_ref[...] = g32_ref[...].astype(BF16)


def _s5_scan(uz, x0, params, d_skip, nb, T):
    M = uz.shape[0]
    pd, qtd, rmat, a1, a2 = params
    nt = pd.shape[0]
    tw = LANES
    sw = x0.shape[1] // nt
    nc = M // (nb * T)
    rows = nb * nc
    kern = functools.partial(_s5_kernel, T=T, nb=nb, nc=nc)
    E = nt * tw
    return pl.pallas_call(
        kern,
        grid=(nt,),
        in_specs=[
            pl.BlockSpec((M, tw), lambda i: (0, i)),
            pl.BlockSpec((nb, sw), lambda i: (0, i)),
            pl.BlockSpec((None, T * tw, tw), lambda i: (i, 0, 0)),
            pl.BlockSpec((None, T * tw, tw), lambda i: (i, 0, 0)),
            pl.BlockSpec((None, (T + 1) * tw, tw), lambda i: (i, 0, 0)),
            pl.BlockSpec((None, 1, sw), lambda i: (i, 0, 0)),
            pl.BlockSpec((None, 1, sw), lambda i: (i, 0, 0)),
            pl.BlockSpec((1, tw), lambda i: (0, i)),
        ],
        out_specs=[
            pl.BlockSpec((M, tw), lambda i: (0, i)),
            pl.BlockSpec((M, tw), lambda i: (0, i)),
            pl.BlockSpec((nb, sw), lambda i: (0, i)),
        ],
        out_shape=[
            jax.ShapeDtypeStruct((M, E), F32),
            jax.ShapeDtypeStruct((M, E), BF16),
            jax.ShapeDtypeStruct((nb, nt * sw), F32),
        ],
        scratch_shapes=[pltpu.VMEM((rows, T * tw), BF16)]
        + [pltpu.VMEM((sw // tw, rows, tw), F32)] * 3,
        compiler_params=_cparams(("parallel",)),
        name="s5_scan",
    )(uz, x0, pd, qtd, jnp.pad(rmat, ((0, 0), (0, tw), (0, 0))), a1, a2, d_skip)


def _glu_epilogue(acc, g32, z, bias):
    return g32 * jax.nn.sigmoid(acc + bias) * _silu(z)


def _rglru_gates(xc, wa_ref, wx_ref, ba_ref, bx_ref, lam_ref):
    xb = xc.astype(BF16)
    r = jax.nn.sigmoid(jnp.dot(xb, wa_ref[...].astype(BF16), preferred_element_type=F32)
                       + ba_ref[...])
    i = jax.nn.sigmoid(jnp.dot(xb, wx_ref[...].astype(BF16), preferred_element_type=F32)
                       + bx_ref[...])
    log_a = -RGLRU_POW * r * _softplus(-lam_ref[...])
    a = jnp.exp(log_a)
    return a, jnp.sqrt(-jnp.tanh(log_a) * (a * a + 1.0)) * (i * xc)


def _rglru_prompt_kernel(u_ref, z_ref, prev_ref, wa_ref, wx_ref, cw_ref, cb_ref, ba_ref, bx_ref,
                         lam_ref, out_ref, hlast_ref,
                         ext_s, a_s, b_s, hs_s, h_s, *, nb, tl, halo):
    t = pl.program_id(1)
    cbw = u_ref.shape[2]

    @pl.when(t == 0)
    def _():
        h_s[...] = jnp.zeros_like(h_s)
        ext_s[:, 0:halo, :] = jnp.zeros((nb, halo, cbw), F32)

    @pl.when(t != 0)
    def _():
        ext_s[:, 0:halo, :] = prev_ref[...]

    u = u_ref[...]
    ext_s[:, halo:, :] = u
    cw = cw_ref[...]
    nk = cw.shape[0]
    xc = cb_ref[...] + cw[nk - 1:nk, :] * u
    for j in range(nk - 1):
        xc = xc + cw[j:j + 1, :] * ext_s[:, pl.ds(halo - (nk - 1) + j, tl), :]
    xc = xc.reshape(nb * tl, cbw)
    a, b = _rglru_gates(xc, wa_ref, wx_ref, ba_ref, bx_ref, lam_ref)
    nl = cbw // LANES
    for g in range(nl):
        a_s[g] = a[:, g * LANES:(g + 1) * LANES]
        b_s[g] = b[:, g * LANES:(g + 1) * LANES]

    def step(r, hcur):
        sl = pl.ds(r, nb, stride=tl)
        new = []
        for g in range(nl):
            hg = a_s[g, sl, :] * hcur[:, g * LANES:(g + 1) * LANES] + b_s[g, sl, :]
            hs_s[g, sl, :] = hg
            new.append(hg)
        return jnp.concatenate(new, axis=1)

    hfin = lax.fori_loop(0, tl, step, h_s[...], unroll=8)
    h_s[...] = hfin
    hlast_ref[...] = hfin
    hs = jnp.concatenate([hs_s[g] for g in range(nl)], axis=1)
    out_ref[...] = (hs.reshape(nb, tl, cbw) * _silu(z_ref[...])).astype(BF16)


def _rglru_prompt(uz3, w_a, w_x, conv_w, conv_b, b_a, b_x, lam):
    B, L, E2 = uz3.shape
    E = E2 // 2
    nblk, cbw = w_a.shape[1], w_a.shape[2]
    tl = _pick(L, 256)
    halo = 8
    nk = conv_w.shape[1]
    kern = functools.partial(_rglru_prompt_kernel, nb=B, tl=tl, halo=halo)
    vec = lambda a: a.reshape(1, E)
    vspec = pl.BlockSpec((1, cbw), lambda n, t: (0, n))
    out, hlast = pl.pallas_call(
        kern,
        grid=(nblk, L // tl),
        in_specs=[
            pl.BlockSpec((B, tl, cbw), lambda n, t: (0, t, n)),
            pl.BlockSpec((B, tl, cbw), lambda n, t: (0, t, nblk + n)),
            pl.BlockSpec((B, halo, cbw), lambda n, t: (0, jnp.maximum(t * (tl // halo) - 1, 0), n)),
            pl.BlockSpec((None, None, cbw, cbw), lambda n, t: (0, n, 0, 0)),
            pl.BlockSpec((None, None, cbw, cbw), lambda n, t: (0, n, 0, 0)),
            pl.BlockSpec((None, nk, cbw), lambda n, t: (0, 0, n)),
            vspec, vspec, vspec, vspec,
        ],
        out_specs=[
            pl.BlockSpec((B, tl, cbw), lambda n, t: (0, t, n)),
            pl.BlockSpec((B, cbw), lambda n, t: (0, n)),
        ],
        out_shape=[jax.ShapeDtypeStruct((B, L, E), BF16), jax.ShapeDtypeStruct((B, E), F32)],
        scratch_shapes=[pltpu.VMEM((B, tl + halo, cbw), F32)]
        + [pltpu.VMEM((cbw // LANES, B * tl, LANES), F32)] * 3 + [
                        pltpu.VMEM((B, cbw), F32)],
        compiler_params=_cparams(("parallel", "arbitrary")),
        name="rglru_prompt",
    )(uz3, uz3, uz3, w_a, w_x, conv_w, vec(conv_b), vec(b_a), vec(b_x), vec(lam))
    return out, hlast


def _rglru_step_kernel(u_ref, z_ref, buf_ref, h0_ref, wa_ref, wx_ref, cw_ref, cb_ref, ba_ref,
                       bx_ref, lam_ref, out_ref, hnew_ref):
    u = u_ref[...]
    cw = cw_ref[...]
    nk = cw.shape[0]
    xc = cb_ref[...] + cw[nk - 1:nk, :] * u
    for j in range(nk - 1):
        xc = xc + cw[j:j + 1, :] * buf_ref[j]
    a, b = _rglru_gates(xc, wa_ref, wx_ref, ba_ref, bx_ref, lam_ref)
    hnew = a * h0_ref[...] + b
    hnew_ref[...] = hnew
    out_ref[...] = (hnew * _silu(z_ref[...])).astype(BF16)


def _rglru_step(uz, buf_t, h0, w_a, w_x, conv_w, conv_b, b_a, b_x, lam):
    B, E2 = uz.shape
    E = E2 // 2
    nblk, cbw = w_a.shape[1], w_a.shape[2]
    nk = conv_w.shape[1]
    vec = lambda a: a.reshape(1, E)
    vspec = pl.BlockSpec((1, cbw), lambda n: (0, n))
    blk = lambda off: pl.BlockSpec((B, cbw), lambda n, off=off: (0, off + n))
    return pl.pallas_call(
        _rglru_step_kernel,
        grid=(nblk,),
        in_specs=[
            blk(0), blk(nblk),
            pl.BlockSpec((nk - 1, B, cbw), lambda n: (0, 0, n)),
            blk(0),
            pl.BlockSpec((None, None, cbw, cbw), lambda n: (0, n, 0, 0)),
            pl.BlockSpec((None, None, cbw, cbw), lambda n: (0, n, 0, 0)),
            pl.BlockSpec((None, nk, cbw), lambda n: (0, 0, n)),
            vspec, vspec, vspec, vspec,
        ],
        out_specs=[blk(0), blk(0)],
        out_shape=[jax.ShapeDtypeStruct((B, E), BF16), jax.ShapeDtypeStruct((B, E), F32)],
        compiler_params=_cparams(("parallel",)),
        name="rglru_step",
    )(uz, uz, buf_t, h0, w_a, w_x, conv_w, vec(conv_b), vec(b_a), vec(b_x), vec(lam))


def _trunk(x, ada_all, states, p, is_prompt):
    B, L, D = x.shape
    M = B * L
    depth = p['ada_w'].shape[0]
    mC, mn, mm, s_re, s_im, r_h, r_conv = states
    heads, dv, dqk = mC.shape[2], mC.shape[3], mC.shape[4]
    G, P = s_re.shape[2], s_re.shape[3]
    E = p['s5_D'].shape[1]
    x3 = x if is_prompt else x.reshape(1, M, D)
    outs = dict(C=[], n=[], m=[], re=[], im=[], h=[], conv=[])
    c_all = None
    for i in range(depth):
        ada = ada_all[i]
        ada = ada.reshape(B, 1, 3 * D) if is_prompt else ada.reshape(1, M, 3 * D)
        hn = _prenorm(x3, p['norm_pre'], i, ada).reshape(M, D)
        kind, j = i % 3, i // 3
        if kind == 0:
            w_in = p['mlstm_w_in']
            n_main = 2 * heads * dqk + 3 * heads * dv
            proj = _matmul(hn, w_in, j, n_out=n_main, name="mm_mlstm_in")
            w_if = jnp.pad(w_in[j, :, n_main:], ((0, 0), (0, LANES - 2 * heads)))[None]
            gates = _matmul(hn, w_if, 0, name="mm_mlstm_gates")
            bias = jnp.pad(p['mlstm_b_if'][j], (0, LANES - 2 * heads)).reshape(1, LANES)
            if is_prompt:
                gated, c1, n1, m1 = _mlstm_prompt(
                    proj.reshape(B, L, n_main), gates.reshape(B, L, LANES), bias,
                    p['mlstm_head_norm'], j, heads, dqk, dv)
                gated = gated.reshape(M, heads * dv)
                n1 = n1.reshape(B, heads, dqk)
                m1 = m1[:, :, 0, 0]
                outs['C'].append(c1)
            else:
                gated, c_all, n1, m1 = _mlstm_step(proj, gates, bias, p['mlstm_head_norm'], j,
                                                   mC, mn, mm, heads, dqk, dv, c_all)
            outs['n'].append(n1); outs['m'].append(m1)
            out = _matmul(gated, p['mlstm_w_out'], j, name="mm_mlstm_out")
        elif kind == 1:
            uz = _matmul(hn, p['s5_w_in'], j, name="mm_s5_in")
            T = S5_CHUNK if L % S5_CHUNK == 0 else 1
            params = _s5_params(p['s5_lambda_re'][j], p['s5_lambda_im'][j], p['s5_log_dt'][j],
                                p['s5_B_re'][j], p['s5_B_im'][j], p['s5_C_re'][j], p['s5_C_im'][j], T)
            if is_prompt:
                x0 = jnp.zeros((B, G * 2 * P), F32)
            else:
                x0 = jnp.stack([s_re[j], s_im[j]], axis=2).reshape(B, G * 2 * P)
            g32, g16, xf = _s5_scan(uz, x0, params, p['s5_D'][j].reshape(1, E), B, T)
            xf = xf.reshape(B, G, 2, P)
            outs['re'].append(xf[:, :, 0]); outs['im'].append(xf[:, :, 1])
            y = _matmul(g16, p['s5_w_glu'], j, out_dtype=BF16, epilogue=_glu_epilogue, name="mm_s5_glu",
                        extras=[(g32, 'tile', 0), (uz, 'tile', E),
                                (p['s5_b_glu'][j].reshape(1, E), 'row', 0)])
            out = _matmul(y, p['s5_w_out'], j, name="mm_s5_out")
        else:
            uz = _matmul(hn, p['rglru_w_in'], j, name="mm_rglru_in")
            args = (p['rglru_w_a'][j:j + 1], p['rglru_w_x'][j:j + 1], p['rglru_conv_w'][j:j + 1],
                    p['rglru_conv_b'][j], p['rglru_b_a'][j], p['rglru_b_x'][j], p['rglru_lambda'][j])
            if is_prompt:
                gated, h1 = _rglru_prompt(uz.reshape(B, L, 2 * E), *args)
                gated = gated.reshape(M, E)
                nk = p['rglru_conv_w'].shape[1]
                buf = uz.reshape(B, L, 2 * E)[:, L - (nk - 1):, :E]
            else:
                buf_t = r_conv[j].transpose(1, 0, 2)
                gated, h1 = _rglru_step(uz, buf_t, r_h[j], *args)
                buf = jnp.concatenate([r_conv[j][:, 1:], uz[:, None, :E]], axis=1)
            outs['h'].append(h1); outs['conv'].append(buf)
            out = _matmul(gated, p['rglru_w_out'], j, name="mm_rglru_out")
        x3 = _postnorm(x3, out, p['norm_post'], i, ada)
    st = lambda k: jnp.stack(outs[k])
    c_out = st('C') if is_prompt else c_all
    return (x3.reshape(B, L, D), c_out, st('n'), st('m'), st('re'), st('im'), st('h'), st('conv'))


def kernel(x_prompt, x_sample, c_prompt, c_sample, state_mlstm_C, state_mlstm_n, state_mlstm_m,
           state_s5_re, state_s5_im, state_rglru_h, state_rglru_conv,
           norm_pre, norm_post, ada_w, ada_b,
           mlstm_w_in, mlstm_b_if, mlstm_head_norm, mlstm_w_out,
           s5_w_in, s5_lambda_re, s5_lambda_im, s5_log_dt, s5_B_re, s5_B_im, s5_C_re, s5_C_im,
           s5_D, s5_w_glu, s5_b_glu, s5_w_out,
           rglru_w_in, rglru_conv_w, rglru_conv_b, rglru_w_a, rglru_b_a, rglru_w_x, rglru_b_x,
           rglru_lambda, rglru_w_out):
    p = dict(norm_pre=norm_pre, norm_post=norm_post, ada_w=ada_w, ada_b=ada_b,
             mlstm_w_in=mlstm_w_in, mlstm_b_if=mlstm_b_if, mlstm_head_norm=mlstm_head_norm,
             mlstm_w_out=mlstm_w_out,
             s5_w_in=s5_w_in, s5_lambda_re=s5_lambda_re, s5_lambda_im=s5_lambda_im,
             s5_log_dt=s5_log_dt, s5_B_re=s5_B_re, s5_B_im=s5_B_im, s5_C_re=s5_C_re,
             s5_C_im=s5_C_im, s5_D=s5_D, s5_w_glu=s5_w_glu, s5_b_glu=s5_b_glu, s5_w_out=s5_w_out,
             rglru_w_in=rglru_w_in, rglru_conv_w=rglru_conv_w, rglru_conv_b=rglru_conv_b,
             rglru_w_a=rglru_w_a, rglru_b_a=rglru_b_a, rglru_w_x=rglru_w_x, rglru_b_x=rglru_b_x,
             rglru_lambda=rglru_lambda, rglru_w_out=rglru_w_out)
    bp = x_prompt.shape[0]
    bs = x_sample.shape[0]
    s_all = bp + bs
    s_pad = -(-s_all // 16) * 16
    c_all = jnp.pad(jnp.concatenate([c_prompt, c_sample], axis=0), ((0, s_pad - s_all), (0, 0)))
    ada_all = _ada_all(c_all, ada_w, ada_b)
    states = (state_mlstm_C, state_mlstm_n, state_mlstm_m, state_s5_re, state_s5_im,
              state_rglru_h, state_rglru_conv)
    res_p = _trunk(x_prompt, ada_all[:, :bp], states, p, True)
    res_s = _trunk(x_sample, ada_all[:, bp:s_all], states, p, False)
    return (res_p[0], res_s[0]) + tuple(res_p[1:]) + tuple(res_s[1:])
```

```python
import functools
import math

import jax
import jax.numpy as jnp
from jax import lax
from jax.experimental import pallas as pl
from jax.experimental.pallas import tpu as pltpu

F32 = jnp.float32
BF16 = jnp.bfloat16
RMS_EPS = 1e-6
DT_FLOOR_LAMBDA = -1e-4
RGLRU_POW = 8.0
LANES = 128
S5_TILE_GROUPS = 8
S5_CHUNK = 16
MLSTM_CHUNK = 256
V7X_VMEM_BYTES = 64 * 1024 * 1024
VMEM_LIMIT = V7X_VMEM_BYTES * 7 // 8
MM_X_PANEL_BYTES = V7X_VMEM_BYTES // 4
MM_W_PANEL_BYTES = V7X_VMEM_BYTES // 8

_NT = (((1,), (1,)), ((), ()))
_TN = (((0,), (0,)), ((), ()))


def _cparams(sem):
    return pltpu.CompilerParams(dimension_semantics=sem, vmem_limit_bytes=VMEM_LIMIT)


def _silu(x):
    return x * jax.nn.sigmoid(x)


def _log_sigmoid(x):
    return jnp.minimum(x, 0.0) - jnp.log1p(jnp.exp(-jnp.abs(x)))


def _softplus(x):
    return jnp.maximum(x, 0.0) + jnp.log1p(jnp.exp(-jnp.abs(x)))


def _gelu_tanh(x):
    c = math.sqrt(2.0 / math.pi)
    return 0.5 * x * (1.0 + jnp.tanh(c * (x + 0.044715 * (x * x * x))))


def _pick(n, pref):
    if n <= pref:
        return n
    t = pref
    while n % t:
        t //= 2
    return t


def _mm_kernel(x_ref, w_ref, *rest, epilogue, n_extra):
    extra = rest[:n_extra]
    o_ref = rest[n_extra]
    acc = jnp.dot(x_ref[...], w_ref[...].astype(BF16), preferred_element_type=F32)
    if epilogue is not None:
        acc = epilogue(acc, *[e[...] for e in extra])
    o_ref[...] = acc.astype(o_ref.dtype)


def _mm_tiles(M, K, N):
    tm = _pick(M, max(MM_X_PANEL_BYTES // (2 * K), 16))
    tn = _pick(N, max(MM_W_PANEL_BYTES // (4 * K), LANES))
    return tm, tn


def _matmul(x, w, layer, *, n_out=None, out_dtype=F32, epilogue=None, extras=(), name="mm"):
    assert x.dtype == BF16
    M, K = x.shape
    N = w.shape[2] if n_out is None else n_out
    tm, tn = _mm_tiles(M, K, N)
    in_specs = [
        pl.BlockSpec((tm, K), lambda i, j: (i, 0), pipeline_mode=pl.Buffered(1)),
        pl.BlockSpec((None, K, tn), lambda i, j: (layer, 0, j)),
    ]
    args = [x, w]
    for arr, kind, off in extras:
        if kind == "tile":
            in_specs.append(pl.BlockSpec((tm, tn), lambda i, j, off=off: (i, j + off // tn)))
        else:
            in_specs.append(pl.BlockSpec((1, tn), lambda i, j, off=off: (0, j + off // tn)))
        args.append(arr)
    kern = functools.partial(_mm_kernel, epilogue=epilogue, n_extra=len(extras))
    return pl.pallas_call(
        kern,
        grid=(M // tm, N // tn),
        in_specs=in_specs,
        out_specs=pl.BlockSpec((tm, tn), lambda i, j: (i, j)),
        out_shape=jax.ShapeDtypeStruct((M, N), out_dtype),
        compiler_params=_cparams(("parallel", "arbitrary")),
        name=name,
    )(*args)


def _ada_kernel(c_ref, w_ref, b_ref, o_ref, acc_ref):
    k = pl.program_id(2)
    part = jnp.dot(_silu(c_ref[...]).astype(BF16), w_ref[...].astype(BF16),
                   preferred_element_type=F32)

    @pl.when(k == 0)
    def _():
        acc_ref[...] = part

    @pl.when(k != 0)
    def _():
        acc_ref[...] += part

    @pl.when(k == pl.num_programs(2) - 1)
    def _():
        o_ref[...] = acc_ref[...] + b_ref[...]


def _ada_all(c, ada_w, ada_b):
    S, D = c.shape
    depth, _, N = ada_w.shape
    tn, tk = _pick(N, 1024), _pick(D, 2048)
    return pl.pallas_call(
        _ada_kernel,
        grid=(depth, N // tn, D // tk),
        in_specs=[
            pl.BlockSpec((S, tk), lambda l, j, k: (0, k)),
            pl.BlockSpec((None, tk, tn), lambda l, j, k: (l, k, j)),
            pl.BlockSpec((None, 1, tn), lambda l, j, k: (l, 0, j)),
        ],
        out_specs=pl.BlockSpec((None, S, tn), lambda l, j, k: (l, 0, j)),
        out_shape=jax.ShapeDtypeStruct((depth, S, N), F32),
        scratch_shapes=[pltpu.VMEM((S, tn), F32)],
        compiler_params=_cparams(("parallel", "parallel", "arbitrary")),
        name="adaln",
    )(c, ada_w, ada_b.reshape(depth, 1, N))


def _rms(x, g):
    return x * lax.rsqrt(jnp.mean(x * x, axis=-1, keepdims=True) + RMS_EPS) * g


def _prenorm_kernel(x_ref, g_ref, shift_ref, scale_ref, o_ref):
    y = _rms(x_ref[...], g_ref[...])
    o_ref[...] = (y * (1.0 + scale_ref[...]) + shift_ref[...]).astype(o_ref.dtype)


def _postnorm_kernel(x_ref, out_ref, g_ref, gate_ref, o_ref):
    o_ref[...] = x_ref[...] + gate_ref[...] * _rms(out_ref[...], g_ref[...])


def _ada_specs(ada, tl, D):
    la = ada.shape[1]
    if la == 1:
        return [pl.BlockSpec((None, 1, D), lambda b, t, c=c: (b, 0, c)) for c in range(3)]
    return [pl.BlockSpec((None, tl, D), lambda b, t, c=c: (b, t, c)) for c in range(3)]


def _prenorm(x, g, layer, ada):
    Bx, Lx, D = x.shape
    tl = _pick(Lx, 256)
    shift_spec, scale_spec, _ = _ada_specs(ada, tl, D)
    return pl.pallas_call(
        _prenorm_kernel,
        grid=(Bx, Lx // tl),
        in_specs=[
            pl.BlockSpec((None, tl, D), lambda b, t: (b, t, 0)),
            pl.BlockSpec((None, 1, D), lambda b, t: (layer, 0, 0)),
            shift_spec, scale_spec,
        ],
        out_specs=pl.BlockSpec((None, tl, D), lambda b, t: (b, t, 0)),
        out_shape=jax.ShapeDtypeStruct((Bx, Lx, D), BF16),
        compiler_params=_cparams(("parallel", "parallel")),
        name="prenorm",
    )(x, g.reshape(g.shape[0], 1, D), ada, ada)


def _postnorm(x, out, g, layer, ada):
    Bx, Lx, D = x.shape
    tl = _pick(Lx, 256)
    gate_spec = _ada_specs(ada, tl, D)[2]
    return pl.pallas_call(
        _postnorm_kernel,
        grid=(Bx, Lx // tl),
        in_specs=[
            pl.BlockSpec((None, tl, D), lambda b, t: (b, t, 0)),
            pl.BlockSpec((None, tl, D), lambda b, t: (b, t, 0)),
            pl.BlockSpec((None, 1, D), lambda b, t: (layer, 0, 0)),
            gate_spec,
        ],
        out_specs=pl.BlockSpec((None, tl, D), lambda b, t: (b, t, 0)),
        out_shape=jax.ShapeDtypeStruct((Bx, Lx, D), F32),
        compiler_params=_cparams(("parallel", "parallel")),
        name="postnorm",
    )(x, out.reshape(Bx, Lx, D), g.reshape(g.shape[0], 1, D), ada)


def _mlstm_gate_kernel(g_ref, bias_ref, colli_ref, colb_ref, rowli_ref, rowb_ref):
    g = g_ref[...] + bias_ref[...]
    lc = g.shape[0]
    lf = _log_sigmoid(g)
    tri = (lax.broadcasted_iota(jnp.int32, (lc, lc), 0)
           >= lax.broadcasted_iota(jnp.int32, (lc, lc), 1)).astype(F32)
    bcum = jnp.dot(tri, lf, preferred_element_type=F32, precision=lax.Precision.HIGHEST)
    colli_ref[...] = g
    colb_ref[...] = bcum
    rowli_ref[...] = g.T
    rowb_ref[...] = bcum.T


def _mlstm_gates(gates, bias, lc):
    B, L, W = gates.shape
    col = jax.ShapeDtypeStruct((B, L, W), F32)
    row = jax.ShapeDtypeStruct((B, W, L), F32)
    cspec = pl.BlockSpec((None, lc, W), lambda b, c: (b, c, 0))
    rspec = pl.BlockSpec((None, W, lc), lambda b, c: (b, 0, c))
    return pl.pallas_call(
        _mlstm_gate_kernel,
        grid=(B, L // lc),
        in_specs=[cspec, pl.BlockSpec((1, W), lambda b, c: (0, 0))],
        out_specs=[cspec, cspec, rspec, rspec],
        out_shape=[col, col, row, row],
        compiler_params=_cparams(("parallel", "parallel")),
        name="mlstm_gates",
    )(gates, bias)


def _head_gate(hh, o, z, gh):
    hs = jax.nn.sigmoid(o) * hh
    hs = hs * lax.rsqrt(jnp.mean(hs * hs, axis=-1, keepdims=True) + RMS_EPS)
    return (hs * gh * _silu(z)).astype(BF16)


def _mlstm_prompt_kernel(q_ref, k_ref, v_ref, o_ref, z_ref, colli_ref, colb_ref,
                         rowli_ref, rowb_ref, gh_ref,
                         out_ref, cst_ref, nst_ref, mst_ref,
                         c_s, n_s, m_s, *, heads):
    h = pl.program_id(1)
    c = pl.program_id(2)
    lc, dqk = q_ref.shape

    @pl.when(c == 0)
    def _():
        c_s[...] = jnp.zeros_like(c_s)
        n_s[...] = jnp.zeros_like(n_s)
        m_s[...] = jnp.zeros_like(m_s)

    q = q_ref[...]
    k = k_ref[...] * (dqk ** -0.5)
    v = v_ref[...]
    lane = lax.broadcasted_iota(jnp.int32, colb_ref.shape, 1)
    b_col = jnp.sum(jnp.where(lane == heads + h, colb_ref[...], 0.0), axis=1, keepdims=True)
    li_col = jnp.sum(jnp.where(lane == h, colli_ref[...], 0.0), axis=1, keepdims=True)
    b_row = rowb_ref[pl.ds(heads + h, 1), :]
    li_row = rowli_ref[pl.ds(h, 1), :]
    m_prev = m_s[...]

    causal = (lax.broadcasted_iota(jnp.int32, (lc, lc), 1)
              <= lax.broadcasted_iota(jnp.int32, (lc, lc), 0))
    dmat = jnp.where(causal, b_col - b_row + li_row, -jnp.inf)
    inter = b_col + m_prev
    m_t = jnp.maximum(inter, jnp.max(dmat, axis=1, keepdims=True))
    qb, kb, vb = q.astype(BF16), k.astype(BF16), v.astype(BF16)
    s = lax.dot_general(qb, kb, _NT, preferred_element_type=F32) * jnp.exp(dmat - m_t)
    w_inter = jnp.exp(inter - m_t)
    cmat = c_s[...]
    qc = lax.dot_general(qb, cmat.astype(BF16), _NT, preferred_element_type=F32)
    num = jnp.dot(s.astype(BF16), vb, preferred_element_type=F32) + w_inter * qc
    nvec = n_s[...]
    den = (jnp.sum(s, axis=1, keepdims=True)
           + w_inter * jnp.sum(q * nvec, axis=1, keepdims=True))
    hh = num / jnp.maximum(jnp.abs(den), jnp.exp(-m_t))
    out_ref[...] = _head_gate(hh, o_ref[...], z_ref[...], gh_ref[...])

    b_last = b_col[lc - 1:lc, :]
    m_new = jnp.maximum(b_last + m_prev,
                        jnp.max(b_last - b_row + li_row, axis=1, keepdims=True))
    decay = jnp.exp(b_last + m_prev - m_new)
    w_col = jnp.exp(b_last - b_col + li_col - m_new)
    c_new = decay * cmat + lax.dot_general((v * w_col).astype(BF16), kb, _TN,
                                           preferred_element_type=F32)
    n_new = decay * nvec + jnp.sum(k * w_col, axis=0, keepdims=True)
    c_s[...] = c_new
    n_s[...] = n_new
    m_s[...] = m_new

    @pl.when(c == pl.num_programs(2) - 1)
    def _():
        cst_ref[...] = c_new
        nst_ref[...] = n_new
        mst_ref[...] = jnp.broadcast_to(m_new, mst_ref.shape)


def _mlstm_prompt(proj, gates, bias, g_head, layer, heads, dqk, dv):
    B, L, _ = proj.shape
    lc = _pick(L, MLSTM_CHUNK)
    colli, colb, rowli, rowb = _mlstm_gates(gates, bias, lc)
    W = gates.shape[2]
    kq = heads * dqk // dqk
    kv = 2 * heads * dqk // dv
    tok = lambda blk, off: pl.BlockSpec((None, lc, blk), lambda b, h, c, off=off: (b, c, off + h))
    cspec = pl.BlockSpec((None, lc, W), lambda b, h, c: (b, c, 0))
    rspec = pl.BlockSpec((None, W, lc), lambda b, h, c: (b, 0, c))
    kern = functools.partial(_mlstm_prompt_kernel, heads=heads)
    return pl.pallas_call(
        kern,
        grid=(B, heads, L // lc),
        in_specs=[
            tok(dqk, 0), tok(dqk, kq), tok(dv, kv), tok(dv, kv + heads), tok(dv, kv + 2 * heads),
            cspec, cspec, rspec, rspec,
            pl.BlockSpec((None, 1, dv), lambda b, h, c: (layer, 0, h)),
        ],
        out_specs=[
            pl.BlockSpec((None, lc, dv), lambda b, h, c: (b, c, h)),
            pl.BlockSpec((None, None, dv, dqk), lambda b, h, c: (b, h, 0, 0)),
            pl.BlockSpec((None, None, 1, dqk), lambda b, h, c: (b, h, 0, 0)),
            pl.BlockSpec((None, None, 1, LANES), lambda b, h, c: (b, h, 0, 0)),
        ],
        out_shape=[
            jax.ShapeDtypeStruct((B, L, heads * dv), BF16),
            jax.ShapeDtypeStruct((B, heads, dv, dqk), F32),
            jax.ShapeDtypeStruct((B, heads, 1, dqk), F32),
            jax.ShapeDtypeStruct((B, heads, 1, LANES), F32),
        ],
        scratch_shapes=[pltpu.VMEM((dv, dqk), F32), pltpu.VMEM((1, dqk), F32),
                        pltpu.VMEM((1, 1), F32)],
        compiler_params=_cparams(("parallel", "parallel", "arbitrary")),
        name="mlstm_prompt",
    )(proj, proj, proj, proj, proj, colli, colb, rowli, rowb,
      g_head.reshape(g_head.shape[0], 1, heads * dv))


def _mlstm_step_coef_kernel(proj_ref, g_ref, bias_ref, n_ref, m_ref,
                            s_ref, wi_ref, mnew_ref, dn_ref, nnew_ref, kw_ref, *, heads, dqk):
    g = g_ref[...] + bias_ref[...]
    lf = _log_sigmoid(g)
    m = m_ref[...]
    for h in range(heads):
        q = proj_ref[:, h * dqk:(h + 1) * dqk]
        k = proj_ref[:, (heads + h) * dqk:(heads + h + 1) * dqk] * (dqk ** -0.5)
        nv = n_ref[:, h * dqk:(h + 1) * dqk]
        logi = g[:, h:h + 1]
        inter = lf[:, heads + h:heads + h + 1] + m[:, h:h + 1]
        m_t = jnp.maximum(inter, logi)
        w_s = jnp.exp(logi - m_t)
        w_inter = jnp.exp(inter - m_t)
        s = jnp.sum(q * k, axis=1, keepdims=True) * w_s
        den = s + w_inter * jnp.sum(nv * q, axis=1, keepdims=True)
        s_ref[:, h:h + 1] = s
        wi_ref[:, h:h + 1] = w_inter
        mnew_ref[:, h:h + 1] = m_t
        dn_ref[:, h:h + 1] = jnp.maximum(jnp.abs(den), jnp.exp(-m_t))
        nnew_ref[:, h * dqk:(h + 1) * dqk] = w_inter * nv + w_s * k
        kw_ref[:, h * dqk:(h + 1) * dqk] = w_s * k


def _mlstm_step_state_kernel(decay_ref, c_ref, qt_ref, kw_ref, vt_ref, cnew_ref, cq_ref, *, bb):
    h = pl.program_id(0)
    bi = pl.program_id(1)

    @pl.when(bi == 0)
    def _():
        cq_ref[...] = jnp.zeros_like(cq_ref)

    qt = qt_ref[...].astype(BF16)
    kw = kw_ref[...].astype(BF16)
    vt = vt_ref[...]
    lane = lax.broadcasted_iota(jnp.int32, vt.shape, 1)
    acc = jnp.zeros(vt.shape, F32)
    for i in range(bb):
        b = bi * bb + i
        cmat = c_ref[i]
        sel = lane == b
        r = jnp.dot(cmat.astype(BF16), qt, preferred_element_type=F32)
        acc = acc + jnp.where(sel, r, 0.0)
        outer = jnp.dot(jnp.where(sel, vt, 0.0).astype(BF16), kw, preferred_element_type=F32)
        cnew_ref[i] = decay_ref[h, b] * cmat + outer
    cq_ref[...] += acc


def _mlstm_step_out_kernel(s_ref, wi_ref, dn_ref, cq_ref, v_ref, o_ref, z_ref, gh_ref, out_ref):
    h = pl.program_id(0)
    lane = lax.broadcasted_iota(jnp.int32, s_ref.shape, 1)
    col = lambda ref: jnp.sum(jnp.where(lane == h, ref[...], 0.0), axis=1, keepdims=True)
    hh = (col(s_ref) * v_ref[...] + col(wi_ref) * cq_ref[...]) / col(dn_ref)
    out_ref[...] = _head_gate(hh, o_ref[...], z_ref[...], gh_ref[...])


def _mlstm_step(proj, gates, bias, g_head, layer, c_state, n_state, m_state, heads, dqk, dv):
    B = proj.shape[0]
    W = gates.shape[1]
    qk_w = heads * dqk
    small = jax.ShapeDtypeStruct((B, heads), F32)
    wide = jax.ShapeDtypeStruct((B, qk_w), F32)
    full = lambda shp: pl.BlockSpec(shp, lambda i: (0,) * len(shp))
    s, w_inter, m_new, denom, n_new, kw = pl.pallas_call(
        functools.partial(_mlstm_step_coef_kernel, heads=heads, dqk=dqk),
        grid=(1,),
        in_specs=[full((B, 2 * qk_w)), full((B, W)), full((1, W)), full((B, qk_w)),
                  full((B, heads))],
        out_specs=[full((B, heads))] * 4 + [full((B, qk_w))] * 2,
        out_shape=[small] * 4 + [wide] * 2,
        compiler_params=_cparams(("arbitrary",)),
        name="mlstm_step_coef",
    )(proj, gates, bias, n_state[layer].reshape(B, qk_w), m_state[layer])

    q_t = proj[:, :qk_w].reshape(B, heads, dqk).transpose(1, 2, 0)
    v_t = proj[:, 2 * qk_w:2 * qk_w + heads * dv].reshape(B, heads, dv).transpose(1, 2, 0)
    kw_h = kw.reshape(B, heads, dqk).transpose(1, 0, 2)
    bb = _pick(B, 8)
    c_new, cq_t = pl.pallas_call(
        functools.partial(_mlstm_step_state_kernel, bb=bb),
        grid=(heads, B // bb),
        in_specs=[
            pl.BlockSpec(memory_space=pltpu.SMEM),
            pl.BlockSpec((None, bb, None, dv, dqk), lambda h, i: (layer, i, h, 0, 0)),
            pl.BlockSpec((None, dqk, B), lambda h, i: (h, 0, 0)),
            pl.BlockSpec((None, B, dqk), lambda h, i: (h, 0, 0)),
            pl.BlockSpec((None, dv, B), lambda h, i: (h, 0, 0)),
        ],
        out_specs=[
            pl.BlockSpec((bb, None, dv, dqk), lambda h, i: (i, h, 0, 0)),
            pl.BlockSpec((None, dv, B), lambda h, i: (h, 0, 0)),
        ],
        out_shape=[jax.ShapeDtypeStruct((B, heads, dv, dqk), F32),
                   jax.ShapeDtypeStruct((heads, dv, B), F32)],
        compiler_params=_cparams(("parallel", "arbitrary")),
        name="mlstm_step_state",
    )(w_inter.T, c_state, q_t, kw_h, v_t)

    cq = cq_t.transpose(2, 0, 1).reshape(B, heads * dv)
    kv = 2 * qk_w // dv
    hblk = lambda off: pl.BlockSpec((B, dv), lambda h, off=off: (0, off + h))
    sm = pl.BlockSpec((B, heads), lambda h: (0, 0))
    gated = pl.pallas_call(
        _mlstm_step_out_kernel,
        grid=(heads,),
        in_specs=[sm, sm, sm, hblk(0), hblk(kv), hblk(kv + heads), hblk(kv + 2 * heads),
                  pl.BlockSpec((None, 1, dv), lambda h: (layer, 0, h))],
        out_specs=hblk(0),
        out_shape=jax.ShapeDtypeStruct((B, heads * dv), BF16),
        compiler_params=_cparams(("parallel",)),
        name="mlstm_step_out",
    )(s, w_inter, denom, cq, proj, proj, proj, g_head.reshape(g_head.shape[0], 1, heads * dv))
    return gated, c_new, n_new.reshape(B, heads, dqk), m_new


def _s5_params(lam_re, lam_im, log_dt, b_re, b_im, c_re, c_im, T):
    G, P = lam_re.shape
    GC = b_re.shape[2]
    tg = S5_TILE_GROUPS
    nt = G // tg
    lr = jnp.minimum(lam_re.astype(F32), DT_FLOOR_LAMBDA)
    li = lam_im.astype(F32)
    dt = jnp.exp(log_dt.astype(F32))[:, None]
    mag = jnp.exp(lr * dt)
    ar = mag * jnp.cos(li * dt)
    ai = mag * jnp.sin(li * dt)
    nr, ni = ar - 1.0, ai
    den = lr * lr + li * li
    q_re = (nr * lr + ni * li) / den
    q_im = (ni * lr - nr * li) / den
    bbr = q_re[..., None] * b_re.astype(F32) - q_im[..., None] * b_im.astype(F32)
    bbi = q_re[..., None] * b_im.astype(F32) + q_im[..., None] * b_re.astype(F32)
    t = jnp.concatenate([jnp.arange(T + 1, dtype=F32), jnp.arange(T - 1, -1, -1, dtype=F32)])
    t = t[:, None, None]
    mag_t = jnp.exp(t * (lr * dt))
    pw_re = mag_t * jnp.cos(t * (li * dt))
    pw_im = mag_t * jnp.sin(t * (li * dt))
    pw_re, pw_im, bbr, bbi = lax.optimization_barrier((pw_re, pw_im, bbr, bbi))
    def dbl(x, y):
        return jnp.concatenate([x, y], -1)

    def tiles(x):
        k, _, r, w = x.shape
        return x.reshape(k, nt, tg, r, w).transpose(1, 0, 2, 3, 4)

    rows = tg * GC
    pw_rr = tiles(dbl(pw_re, pw_re)[:, :, None, :])
    pw_ii = tiles(dbl(pw_im, pw_im)[:, :, None, :])
    pw_mi = tiles(dbl(-pw_im, pw_im)[:, :, None, :])
    bbr_t, bbi_t = bbr.transpose(0, 2, 1), bbi.transpose(0, 2, 1)
    bb_a = tiles(dbl(bbr_t, bbi_t)[None])
    bb_b = tiles(dbl(bbi_t, bbr_t)[None])
    cr, ci = c_re.astype(F32), c_im.astype(F32)
    c_a = tiles(dbl(cr, -ci)[None])
    c_b = tiles(dbl(-ci, -cr)[None])
    rev = lambda x: x[:, T + 1:]
    pd = (rev(pw_rr) * bb_a + rev(pw_mi) * bb_b).reshape(nt, T * rows, 2 * P)
    qtd = (pw_rr[:, 1:T + 1] * c_a + pw_ii[:, 1:T + 1] * c_b).reshape(nt, T * rows, 2 * P)
    e_rev = (rev(pw_rr) * c_a + rev(pw_ii) * c_b).reshape(nt, T, rows, 2 * P)
    kfull = jnp.einsum('nxw,ntyw->ntxy', bb_a.reshape(nt, rows, 2 * P), e_rev,
                       precision=lax.Precision.HIGHEST)
    gid = jnp.arange(rows) // GC
    rmat = jnp.where(gid[:, None] == gid[None, :], kfull, 0.0).reshape(nt, T * rows, rows)
    a1 = dbl(pw_re[T], pw_re[T]).reshape(nt, 1, tg * 2 * P)
    a2 = dbl(-pw_im[T], pw_im[T]).reshape(nt, 1, tg * 2 * P)
    return pd, qtd, rmat, a1, a2


def _s5_kernel(u_ref, x0_ref, pd_ref, qtd_ref, r_ref, a1_ref, a2_ref, d_ref,
               g32_ref, g16_ref, xf_ref,
               u2_s, e_s, es_s, xp_s, *, T, nb, nc):
    rows = nb * nc
    tw = u_ref.shape[1]
    sw = x0_ref.shape[1]
    ng = sw // tw
    for j in range(T):
        u2_s[:, j * tw:(j + 1) * tw] = u_ref[pl.ds(j, rows, stride=T), :].astype(BF16)

    def expand(dense):
        rg = (lax.broadcasted_iota(jnp.int32, dense.shape, 0) % tw) // (tw // ng)
        return jnp.concatenate(
            [jnp.where(rg == g, dense, 0.0).astype(BF16) for g in range(ng)], axis=1)

    e = jnp.dot(u2_s[...], expand(pd_ref[...]), preferred_element_type=F32)
    half = tw // 2

    def swap(xg):
        return pltpu.roll(xg, half, 1)

    for g in range(ng):
        eg = e[:, g * tw:(g + 1) * tw]
        e_s[g] = eg
        es_s[g] = swap(eg)

    a1 = a1_ref[...]
    a2 = a2_ref[...]

    def step(kk, carry):
        sl = pl.ds(kk, nb, stride=nc)
        new = []
        for g in range(ng):
            gs = slice(g * tw, (g + 1) * tw)
            xg, xsg = carry[g]
            xp_s[g, sl, :] = xg
            new.append((a1[:, gs] * xg + a2[:, gs] * xsg + e_s[g, sl, :],
                        a1[:, gs] * xsg - a2[:, gs] * xg + es_s[g, sl, :]))
        return tuple(new)

    x0 = x0_ref[...]
    init = tuple((x0[:, g * tw:(g + 1) * tw], swap(x0[:, g * tw:(g + 1) * tw])) for g in range(ng))
    fin = lax.fori_loop(0, nc, step, init, unroll=min(nc, 4))
    xf_ref[...] = jnp.concatenate([f[0] for f in fin], axis=1)

    xprev = jnp.concatenate([xp_s[g].astype(BF16) for g in range(ng)], axis=1)
    yx = lax.dot_general(xprev, expand(qtd_ref[...]), _NT,
                         preferred_element_type=F32)

    def emit(l, loc):
        sl = pl.ds(l, rows, stride=T)
        y = yx[:, l * tw:(l + 1) * tw] + loc + d_ref[...] * u_ref[sl, :]
        g32_ref[sl, :] = _gelu_tanh(y)

    if T % 2 == 0:
        for l in range(0, T, 2):
            rhs = jnp.concatenate([r_ref[(T - 1 - l) * tw:(T + 1) * tw, :],
                                   r_ref[(T - 2 - l) * tw:T * tw, :]], axis=1).astype(BF16)
            loc = jnp.dot(u2_s[:, :(l + 2) * tw], rhs, preferred_element_type=F32)
            emit(l, loc[:, :tw])
            emit(l + 1, loc[:, tw:])
    else:
        for l in range(T):
            loc = jnp.dot(u2_s[:, :(l + 1) * tw], r_ref[(T - 1 - l) * tw:T * tw, :].astype(BF16),
                          preferred_element_type=F32)
            emit(l, loc)
    g16_ref[...] = g32_ref[...].astype(BF16)


def _s5_scan(uz, x0, params, d_skip, nb, T):
    M = uz.shape[0]
    pd, qtd, rmat, a1, a2 = params
    nt = pd.shape[0]
    tw = LANES
    sw = x0.shape[1] // nt
    nc = M // (nb * T)
    rows = nb * nc
    kern = functools.partial(_s5_kernel, T=T, nb=nb, nc=nc)
    E = nt * tw
    return pl.pallas_call(
        kern,
        grid=(nt,),
        in_specs=[
            pl.BlockSpec((M, tw), lambda i: (0, i)),
            pl.BlockSpec((nb, sw), lambda i: (0, i)),
            pl.BlockSpec((None, T * tw, tw), lambda i: (i, 0, 0)),
            pl.BlockSpec((None, T * tw, tw), lambda i: (i, 0, 0)),
            pl.BlockSpec((None, (T + 1) * tw, tw), lambda i: (i, 0, 0)),
            pl.BlockSpec((None, 1, sw), lambda i: (i, 0, 0)),
            pl.BlockSpec((None, 1, sw), lambda i: (i, 0, 0)),
            pl.BlockSpec((1, tw), lambda i: (0, i)),
        ],
        out_specs=[
            pl.BlockSpec((M, tw), lambda i: (0, i)),
            pl.BlockSpec((M, tw), lambda i: (0, i)),
            pl.BlockSpec((nb, sw), lambda i: (0, i)),
        ],
        out_shape=[
            jax.ShapeDtypeStruct((M, E), F32),
            jax.ShapeDtypeStruct((M, E), BF16),
            jax.ShapeDtypeStruct((nb, nt * sw), F32),
        ],
        scratch_shapes=[pltpu.VMEM((rows, T * tw), BF16)]
        + [pltpu.VMEM((sw // tw, rows, tw), F32)] * 3,
        compiler_params=_cparams(("parallel",)),
        name="s5_scan",
    )(uz, x0, pd, qtd, jnp.pad(rmat, ((0, 0), (0, tw), (0, 0))), a1, a2, d_skip)


def _glu_epilogue(acc, g32, z, bias):
    return g32 * jax.nn.sigmoid(acc + bias) * _silu(z)


def _rglru_gates(xc, wa_ref, wx_ref, ba_ref, bx_ref, lam_ref):
    xb = xc.astype(BF16)
    r = jax.nn.sigmoid(jnp.dot(xb, wa_ref[...].astype(BF16), preferred_element_type=F32)
                       + ba_ref[...])
    i = jax.nn.sigmoid(jnp.dot(xb, wx_ref[...].astype(BF16), preferred_element_type=F32)
                       + bx_ref[...])
    log_a = -RGLRU_POW * r * _softplus(-lam_ref[...])
    a = jnp.exp(log_a)
    return a, jnp.sqrt(-jnp.tanh(log_a) * (a * a + 1.0)) * (i * xc)


def _rglru_prompt_kernel(u_ref, z_ref, prev_ref, wa_ref, wx_ref, cw_ref, cb_ref, ba_ref, bx_ref,
                         lam_ref, out_ref, hlast_ref,
                         ext_s, a_s, b_s, hs_s, h_s, *, nb, tl, halo):
    t = pl.program_id(1)
    cbw = u_ref.shape[2]

    @pl.when(t == 0)
    def _():
        h_s[...] = jnp.zeros_like(h_s)
        ext_s[:, 0:halo, :] = jnp.zeros((nb, halo, cbw), F32)

    @pl.when(t != 0)
    def _():
        ext_s[:, 0:halo, :] = prev_ref[...]

    u = u_ref[...]
    ext_s[:, halo:, :] = u
    cw = cw_ref[...]
    nk = cw.shape[0]
    xc = cb_ref[...] + cw[nk - 1:nk, :] * u
    for j in range(nk - 1):
        xc = xc + cw[j:j + 1, :] * ext_s[:, pl.ds(halo - (nk - 1) + j, tl), :]
    xc = xc.reshape(nb * tl, cbw)
    a, b = _rglru_gates(xc, wa_ref, wx_ref, ba_ref, bx_ref, lam_ref)
    nl = cbw // LANES
    for g in range(nl):
        a_s[g] = a[:, g * LANES:(g + 1) * LANES]
        b_s[g] = b[:, g * LANES:(g + 1) * LANES]

    def step(r, hcur):
        sl = pl.ds(r, nb, stride=tl)
        new = []
        for g in range(nl):
            hg = a_s[g, sl, :] * hcur[:, g * LANES:(g + 1) * LANES] + b_s[g, sl, :]
            hs_s[g, sl, :] = hg
            new.append(hg)
        return jnp.concatenate(new, axis=1)

    hfin = lax.fori_loop(0, tl, step, h_s[...], unroll=8)
    h_s[...] = hfin
    hlast_ref[...] = hfin
    hs = jnp.concatenate([hs_s[g] for g in range(nl)], axis=1)
    out_ref[...] = (hs.reshape(nb, tl, cbw) * _silu(z_ref[...])).astype(BF16)


def _rglru_prompt(uz3, w_a, w_x, conv_w, conv_b, b_a, b_x, lam):
    B, L, E2 = uz3.shape
    E = E2 // 2
    nblk, cbw = w_a.shape[1], w_a.shape[2]
    tl = _pick(L, 256)
    halo = 8
    nk = conv_w.shape[1]
    kern = functools.partial(_rglru_prompt_kernel, nb=B, tl=tl, halo=halo)
    vec = lambda a: a.reshape(1, E)
    vspec = pl.BlockSpec((1, cbw), lambda n, t: (0, n))
    out, hlast = pl.pallas_call(
        kern,
        grid=(nblk, L // tl),
        in_specs=[
            pl.BlockSpec((B, tl, cbw), lambda n, t: (0, t, n)),
            pl.BlockSpec((B, tl, cbw), lambda n, t: (0, t, nblk + n)),
            pl.BlockSpec((B, halo, cbw), lambda n, t: (0, jnp.maximum(t * (tl // halo) - 1, 0), n)),
            pl.BlockSpec((None, None, cbw, cbw), lambda n, t: (0, n, 0, 0)),
            pl.BlockSpec((None, None, cbw, cbw), lambda n, t: (0, n, 0, 0)),
            pl.BlockSpec((None, nk, cbw), lambda n, t: (0, 0, n)),
            vspec, vspec, vspec, vspec,
        ],
        out_specs=[
            pl.BlockSpec((B, tl, cbw), lambda n, t: (0, t, n)),
            pl.BlockSpec((B, cbw), lambda n, t: (0, n)),
        ],
        out_shape=[jax.ShapeDtypeStruct((B, L, E), BF16), jax.ShapeDtypeStruct((B, E), F32)],
        scratch_shapes=[pltpu.VMEM((B, tl + halo, cbw), F32)]
        + [pltpu.VMEM((cbw // LANES, B * tl, LANES), F32)] * 3 + [
                        pltpu.VMEM((B, cbw), F32)],
        compiler_params=_cparams(("parallel", "arbitrary")),
        name="rglru_prompt",
    )(uz3, uz3, uz3, w_a, w_x, conv_w, vec(conv_b), vec(b_a), vec(b_x), vec(lam))
    return out, hlast


def _rglru_step_kernel(u_ref, z_ref, buf_ref, h0_ref, wa_ref, wx_ref, cw_ref, cb_ref, ba_ref,
                       bx_ref, lam_ref, out_ref, hnew_ref):
    u = u_ref[...]
    cw = cw_ref[...]
    nk = cw.shape[0]
    xc = cb_ref[...] + cw[nk - 1:nk, :] * u
    for j in range(nk - 1):
        xc = xc + cw[j:j + 1, :] * buf_ref[j]
    a, b = _rglru_gates(xc, wa_ref, wx_ref, ba_ref, bx_ref, lam_ref)
    hnew = a * h0_ref[...] + b
    hnew_ref[...] = hnew
    out_ref[...] = (hnew * _silu(z_ref[...])).astype(BF16)


def _rglru_step(uz, buf_t, h0, w_a, w_x, conv_w, conv_b, b_a, b_x, lam):
    B, E2 = uz.shape
    E = E2 // 2
    nblk, cbw = w_a.shape[1], w_a.shape[2]
    nk = conv_w.shape[1]
    vec = lambda a: a.reshape(1, E)
    vspec = pl.BlockSpec((1, cbw), lambda n: (0, n))
    blk = lambda off: pl.BlockSpec((B, cbw), lambda n, off=off: (0, off + n))
    return pl.pallas_call(
        _rglru_step_kernel,
        grid=(nblk,),
        in_specs=[
            blk(0), blk(nblk),
            pl.BlockSpec((nk - 1, B, cbw), lambda n: (0, 0, n)),
            blk(0),
            pl.BlockSpec((None, None, cbw, cbw), lambda n: (0, n, 0, 0)),
            pl.BlockSpec((None, None, cbw, cbw), lambda n: (0, n, 0, 0)),
            pl.BlockSpec((None, nk, cbw), lambda n: (0, 0, n)),
            vspec, vspec, vspec, vspec,
        ],
        out_specs=[blk(0), blk(0)],
        out_shape=[jax.ShapeDtypeStruct((B, E), BF16), jax.ShapeDtypeStruct((B, E), F32)],
        compiler_params=_cparams(("parallel",)),
        name="rglru_step",
    )(uz, uz, buf_t, h0, w_a, w_x, conv_w, vec(conv_b), vec(b_a), vec(b_x), vec(lam))


def _trunk(x, ada_all, states, p, is_prompt):
    B, L, D = x.shape
    M = B * L
    depth = p['ada_w'].shape[0]
    mC, mn, mm, s_re, s_im, r_h, r_conv = states
    heads, dv, dqk = mC.shape[2], mC.shape[3], mC.shape[4]
    G, P = s_re.shape[2], s_re.shape[3]
    E = p['s5_D'].shape[1]
    x3 = x if is_prompt else x.reshape(1, M, D)
    outs = dict(C=[], n=[], m=[], re=[], im=[], h=[], conv=[])
    for i in range(depth):
        ada = ada_all[i]
        ada = ada.reshape(B, 1, 3 * D) if is_prompt else ada.reshape(1, M, 3 * D)
        hn = _prenorm(x3, p['norm_pre'], i, ada).reshape(M, D)
        kind, j = i % 3, i // 3
        if kind == 0:
            w_in = p['mlstm_w_in']
            n_main = 2 * heads * dqk + 3 * heads * dv
            proj = _matmul(hn, w_in, j, n_out=n_main, name="mm_mlstm_in")
            w_if = jnp.pad(w_in[j, :, n_main:], ((0, 0), (0, LANES - 2 * heads)))[None]
            gates = _matmul(hn, w_if, 0, name="mm_mlstm_gates")
            bias = jnp.pad(p['mlstm_b_if'][j], (0, LANES - 2 * heads)).reshape(1, LANES)
            if is_prompt:
                gated, c1, n1, m1 = _mlstm_prompt(
                    proj.reshape(B, L, n_main), gates.reshape(B, L, LANES), bias,
                    p['mlstm_head_norm'], j, heads, dqk, dv)
                gated = gated.reshape(M, heads * dv)
                n1 = n1.reshape(B, heads, dqk)
                m1 = m1[:, :, 0, 0]
            else:
                gated, c1, n1, m1 = _mlstm_step(proj, gates, bias, p['mlstm_head_norm'], j,
                                                mC, mn, mm, heads, dqk, dv)
            outs['C'].append(c1); outs['n'].append(n1); outs['m'].append(m1)
            out = _matmul(gated, p['mlstm_w_out'], j, name="mm_mlstm_out")
        elif kind == 1:
            uz = _matmul(hn, p['s5_w_in'], j, name="mm_s5_in")
            T = S5_CHUNK if L % S5_CHUNK == 0 else 1
            params = _s5_params(p['s5_lambda_re'][j], p['s5_lambda_im'][j], p['s5_log_dt'][j],
                                p['s5_B_re'][j], p['s5_B_im'][j], p['s5_C_re'][j], p['s5_C_im'][j], T)
            if is_prompt:
                x0 = jnp.zeros((B, G * 2 * P), F32)
            else:
                x0 = jnp.stack([s_re[j], s_im[j]], axis=2).reshape(B, G * 2 * P)
            g32, g16, xf = _s5_scan(uz, x0, params, p['s5_D'][j].reshape(1, E), B, T)
            xf = xf.reshape(B, G, 2, P)
            outs['re'].append(xf[:, :, 0]); outs['im'].append(xf[:, :, 1])
            y = _matmul(g16, p['s5_w_glu'], j, out_dtype=BF16, epilogue=_glu_epilogue, name="mm_s5_glu",
                        extras=[(g32, 'tile', 0), (uz, 'tile', E),
                                (p['s5_b_glu'][j].reshape(1, E), 'row', 0)])
            out = _matmul(y, p['s5_w_out'], j, name="mm_s5_out")
        else:
            uz = _matmul(hn, p['rglru_w_in'], j, name="mm_rglru_in")
            args = (p['rglru_w_a'][j:j + 1], p['rglru_w_x'][j:j + 1], p['rglru_conv_w'][j:j + 1],
                    p['rglru_conv_b'][j], p['rglru_b_a'][j], p['rglru_b_x'][j], p['rglru_lambda'][j])
            if is_prompt:
                gated, h1 = _rglru_prompt(uz.reshape(B, L, 2 * E), *args)
                gated = gated.reshape(M, E)
                nk = p['rglru_conv_w'].shape[1]
                buf = uz.reshape(B, L, 2 * E)[:, L - (nk - 1):, :E]
            else:
                buf_t = r_conv[j].transpose(1, 0, 2)
                gated, h1 = _rglru_step(uz, buf_t, r_h[j], *args)
                buf = jnp.concatenate([r_conv[j][:, 1:], uz[:, None, :E]], axis=1)
            outs['h'].append(h1); outs['conv'].append(buf)
            out = _matmul(gated, p['rglru_w_out'], j, name="mm_rglru_out")
        x3 = _postnorm(x3, out, p['norm_post'], i, ada)
    st = lambda k: jnp.stack(outs[k])
    return (x3.reshape(B, L, D), st('C'), st('n'), st('m'), st('re'), st('im'), st('h'), st('conv'))


def kernel(x_prompt, x_sample, c_prompt, c_sample, state_mlstm_C, state_mlstm_n, state_mlstm_m,
           state_s5_re, state_s5_im, state_rglru_h, state_rglru_conv,
           norm_pre, norm_post, ada_w, ada_b,
           mlstm_w_in, mlstm_b_if, mlstm_head_norm, mlstm_w_out,
           s5_w_in, s5_lambda_re, s5_lambda_im, s5_log_dt, s5_B_re, s5_B_im, s5_C_re, s5_C_im,
           s5_D, s5_w_glu, s5_b_glu, s5_w_out,
           rglru_w_in, rglru_conv_w, rglru_conv_b, rglru_w_a, rglru_b_a, rglru_w_x, rglru_b_x,
           rglru_lambda, rglru_w_out):
    p = dict(norm_pre=norm_pre, norm_post=norm_post, ada_w=ada_w, ada_b=ada_b,
             mlstm_w_in=mlstm_w_in, mlstm_b_if=mlstm_b_if, mlstm_head_norm=mlstm_head_norm,
             mlstm_w_out=mlstm_w_out,
             s5_w_in=s5_w_in, s5_lambda_re=s5_lambda_re, s5_lambda_im=s5_lambda_im,
             s5_log_dt=s5_log_dt, s5_B_re=s5_B_re, s5_B_im=s5_B_im, s5_C_re=s5_C_re,
             s5_C_im=s5_C_im, s5_D=s5_D, s5_w_glu=s5_w_glu, s5_b_glu=s5_b_glu, s5_w_out=s5_w_out,
             rglru_w_in=rglru_w_in, rglru_conv_w=rglru_conv_w, rglru_conv_b=rglru_conv_b,
             rglru_w_a=rglru_w_a, rglru_b_a=rglru_b_a, rglru_w_x=rglru_w_x, rglru_b_x=rglru_b_x,
             rglru_lambda=rglru_lambda, rglru_w_out=rglru_w_out)
    bp = x_prompt.shape[0]
    bs = x_sample.shape[0]
    s_all = bp + bs
    s_pad = -(-s_all // 16) * 16
    c_all = jnp.pad(jnp.concatenate([c_prompt, c_sample], axis=0), ((0, s_pad - s_all), (0, 0)))
    ada_all = _ada_all(c_all, ada_w, ada_b)
    states = (state_mlstm_C, state_mlstm_n, state_mlstm_m, state_s5_re, state_s5_im,
              state_rglru_h, state_rglru_conv)
    res_p = _trunk(x_prompt, ada_all[:, :bp], states, p, True)
    res_s = _trunk(x_sample, ada_all[:, bp:s_all], states, p, False)
    return (res_p[0], res_s[0]) + tuple(res_p[1:]) + tuple(res_s[1:])
```
